```python
import math
import jax, jax.numpy as jnp
from jax import lax
import numpy as np


D_MODEL = 2048
BATCH = 4
SEQ = 4096
DEPTH = 4

FOX_HEADS = 4
FOX_HEAD_DIM = 128
FOX_BLOCK = 128
GDN_HEADS = 6
GDN_HEAD_DIM = 128
GDN_CHUNK = 64
SSM_HEADS = 12
SSM_HEAD_DIM = 64
SSM_STATE = 128
SSM_GROUPS = 2
SSM_CHUNK = 128
CONV_WIDTH = 4
FFN_DIM = 5632
N_EXPERTS = 8
TOP_K = 2
MOE_BLOCK = 256
NORM_EPS = 1e-6
FORGET_BIAS_INIT = 3.0

FOX_WIDTH = FOX_HEADS * FOX_HEAD_DIM
GDN_WIDTH = GDN_HEADS * GDN_HEAD_DIM
SSM_WIDTH = SSM_HEADS * SSM_HEAD_DIM
D_MIX = FOX_WIDTH + GDN_WIDTH + SSM_WIDTH
GDN_CONV_CH = 3 * GDN_WIDTH
SSM_CONV_CH = SSM_WIDTH + 2 * SSM_GROUPS * SSM_STATE
PROJ_SIZES = (FOX_WIDTH, FOX_WIDTH, FOX_WIDTH, FOX_HEADS,
              GDN_CONV_CH, GDN_WIDTH, GDN_HEADS, GDN_HEADS,
              SSM_WIDTH, SSM_CONV_CH, SSM_HEADS)
D_IN_PROJ = sum(PROJ_SIZES)
N_DENSE = (DEPTH + 1) // 2
N_MOE = DEPTH // 2

kernel_name = 'hybrid_fox_gdn_ssd_moe_trunk'


def rms_norm(x, gain):
    xf = x.astype(jnp.float32)
    y = xf * lax.rsqrt(jnp.mean(xf * xf, axis=-1, keepdims=True) + NORM_EPS)
    return (y * gain.astype(jnp.float32)).astype(x.dtype)


def l2_normalize(x):
    xf = x.astype(jnp.float32)
    return xf * lax.rsqrt(jnp.sum(xf * xf, axis=-1, keepdims=True) + NORM_EPS)


def split_cols(u, sizes):
    return jnp.split(u, np.cumsum(sizes)[:-1].tolist(), axis=-1)


def causal_depthwise_conv(u, w, b=None):
    out = lax.conv_general_dilated(
        u, w[:, None, :].astype(u.dtype), window_strides=(1,),
        padding=[(w.shape[0] - 1, 0)], dimension_numbers=('NWC', 'WIO', 'NWC'),
        feature_group_count=u.shape[-1])
    if b is not None:
        out = out + b.astype(out.dtype)
    return out


def forgetting_attention(q, k, v, log_f):
    b, s, h, d = q.shape
    nb = s // FOX_BLOCK
    c = jnp.cumsum(log_f.astype(jnp.float32), axis=1).transpose(0, 2, 1)
    q_blocks = q.reshape(b, nb, FOX_BLOCK, h, d).swapaxes(0, 1)
    c_blocks = c.reshape(b, h, nb, FOX_BLOCK).transpose(2, 0, 1, 3)
    key_pos = jnp.arange(s)
    scale = d ** -0.5

    def block(args):
        q_blk, c_blk, blk = args
        logits = jnp.einsum('bqhd,bkhd->bhqk', q_blk, k).astype(jnp.float32) * scale
        logits = logits + c_blk[..., :, None] - c[:, :, None, :]
        q_pos = blk * FOX_BLOCK + jnp.arange(FOX_BLOCK)
        causal = key_pos[None, :] <= q_pos[:, None]
        probs = jax.nn.softmax(jnp.where(causal, logits, -jnp.inf), axis=-1)
        return jnp.einsum('bhqk,bkhd->bqhd', probs.astype(v.dtype), v)

    out = lax.map(block, (q_blocks, c_blocks, jnp.arange(nb)))
    return out.swapaxes(0, 1).reshape(b, s, h, d)


def to_chunks(t, chunk):
    b, s = t.shape[:2]
    t = t.astype(jnp.float32).reshape((b, s // chunk, chunk) + t.shape[2:])
    return jnp.moveaxis(t, 3, 2)


def gated_delta_rule(q, k, v, g, beta):
    bsz, s, h, dk = k.shape
    dv = v.shape[-1]
    q = to_chunks(q, GDN_CHUNK) * dk ** -0.5
    k, v, g, beta = (to_chunks(t, GDN_CHUNK) for t in (k, v, g, beta))
    g_cum = jnp.cumsum(g, axis=-1)
    incl = jnp.tril(jnp.ones((GDN_CHUNK, GDN_CHUNK), dtype=bool))
    strict = jnp.tril(jnp.ones((GDN_CHUNK, GDN_CHUNK), dtype=bool), -1)
    decay = jnp.exp(jnp.where(incl, g_cum[..., :, None] - g_cum[..., None, :], -jnp.inf))
    k_beta = k * beta[..., None]
    a_strict = jnp.where(strict, jnp.einsum('bnhid,bnhjd->bnhij', k_beta, k) * decay, 0.0)
    rhs = jnp.concatenate([v * beta[..., None], k_beta * jnp.exp(g_cum)[..., None]], axis=-1)
    solved = lax.linalg.triangular_solve(a_strict, rhs, left_side=True, lower=True,
                                         unit_diagonal=True)
    u, w = solved[..., :dv], solved[..., dv:]
    qk = jnp.einsum('bnhid,bnhjd->bnhij', q, k) * decay
    q_dec = q * jnp.exp(g_cum)[..., None]
    k_dec = k * jnp.exp(g_cum[..., -1:] - g_cum)[..., None]
    chunk_dec = jnp.exp(g_cum[..., -1])

    def step(state, inp):
        u_c, w_c, qk_c, qd_c, kd_c, cd_c = inp
        v_new = u_c - jnp.einsum('bhcd,bhde->bhce', w_c, state)
        o = (jnp.einsum('bhcd,bhde->bhce', qd_c, state)
             + jnp.einsum('bhij,bhje->bhie', qk_c, v_new))
        state = state * cd_c[..., None, None] + jnp.einsum('bhcd,bhce->bhde', kd_c, v_new)
        return state, o

    xs = tuple(jnp.moveaxis(t, 1, 0) for t in (u, w, qk, q_dec, k_dec, chunk_dec))
    _, out = lax.scan(step, jnp.zeros((bsz, h, dk, dv), jnp.float32), xs)
    return out.transpose(1, 0, 3, 2, 4).reshape(bsz, s, h, dv)


def ssd_chunked_scan(x, dt, a, b_mat, c_mat):
    bsz, s, h, p = x.shape
    g, n = b_mat.shape[2], b_mat.shape[3]
    e = h // g
    nc, L = s // SSM_CHUNK, SSM_CHUNK
    xdt = (x.astype(jnp.float32) * dt[..., None]).reshape(bsz, nc, L, g, e, p)
    adt = (dt * a).reshape(bsz, nc, L, g, e)
    bc = b_mat.astype(jnp.float32).reshape(bsz, nc, L, g, n)
    cc = c_mat.astype(jnp.float32).reshape(bsz, nc, L, g, n)
    a_cum = jnp.cumsum(adt, axis=2)
    a_cum_t = jnp.moveaxis(a_cum, 2, -1)
    causal = jnp.tril(jnp.ones((L, L), dtype=bool))
    decay = jnp.exp(jnp.where(causal, a_cum_t[..., :, None] - a_cum_t[..., None, :], -jnp.inf))
    cb = jnp.einsum('bclgn,bcsgn->bcgls', cc, bc)
    y_diag = jnp.einsum('bcgls,bcgels,bcsgep->bclgep', cb, decay, xdt)
    decay_to_end = jnp.exp(a_cum[:, :, -1:] - a_cum)
    chunk_states = jnp.einsum('bclgn,bclge,bclgep->bcgepn', bc, decay_to_end, xdt)
    chunk_decay = jnp.exp(a_cum[:, :, -1])

    def step(state, inp):
        cs, cd = inp
        return state * cd[..., None, None] + cs, state

    _, start_states = lax.scan(step, jnp.zeros((bsz, g, e, p, n), jnp.float32),
                               (jnp.moveaxis(chunk_states, 1, 0), jnp.moveaxis(chunk_decay, 1, 0)))
    start_states = jnp.moveaxis(start_states, 0, 1)
    y_off = jnp.einsum('bclgn,bcgepn,bclge->bclgep', cc, start_states, jnp.exp(a_cum))
    return (y_diag + y_off).reshape(bsz, s, h, p)


def hybrid_mixer(h, w_in, fox_f_bias, fox_out_norm, gdn_conv_w, gdn_A_log, gdn_dt_bias,
                 gdn_out_norm, ssm_conv_w, ssm_conv_b, ssm_A_log, ssm_dt_bias, ssm_D,
                 ssm_out_norm, w_out):
    bsz, s, _ = h.shape
    f32 = jnp.float32
    proj = h @ w_in
    fq, fk, fv, ff, gqkv, gz, gb, ga, sz, sxbc, sdt = split_cols(proj, PROJ_SIZES)

    log_f = jax.nn.log_sigmoid(ff.astype(f32) + fox_f_bias.astype(f32))
    o_fox = forgetting_attention(fq.reshape(bsz, s, FOX_HEADS, FOX_HEAD_DIM),
                                 fk.reshape(bsz, s, FOX_HEADS, FOX_HEAD_DIM),
                                 fv.reshape(bsz, s, FOX_HEADS, FOX_HEAD_DIM), log_f)
    o_fox = rms_norm(o_fox, fox_out_norm).reshape(bsz, s, FOX_WIDTH)

    gqkv = jax.nn.silu(causal_depthwise_conv(gqkv, gdn_conv_w))
    gq, gk, gv = split_cols(gqkv, (GDN_WIDTH, GDN_WIDTH, GDN_WIDTH))
    gq = l2_normalize(gq.reshape(bsz, s, GDN_HEADS, GDN_HEAD_DIM))
    gk = l2_normalize(gk.reshape(bsz, s, GDN_HEADS, GDN_HEAD_DIM))
    beta = jax.nn.sigmoid(gb.astype(f32))
    g_log = -jnp.exp(gdn_A_log.astype(f32)) * jax.nn.softplus(ga.astype(f32) + gdn_dt_bias.astype(f32))
    o_gdn = gated_delta_rule(gq, gk, gv.reshape(bsz, s, GDN_HEADS, GDN_HEAD_DIM), g_log, beta)
    o_gdn = rms_norm(o_gdn, gdn_out_norm) * jax.nn.silu(
        gz.reshape(bsz, s, GDN_HEADS, GDN_HEAD_DIM).astype(f32))
    o_gdn = o_gdn.reshape(bsz, s, GDN_WIDTH)

    sxbc = jax.nn.silu(causal_depthwise_conv(sxbc, ssm_conv_w, ssm_conv_b))
    sx, sb, sc = split_cols(sxbc, (SSM_WIDTH, SSM_GROUPS * SSM_STATE, SSM_GROUPS * SSM_STATE))
    sx = sx.reshape(bsz, s, SSM_HEADS, SSM_HEAD_DIM)
    dt = jax.nn.softplus(sdt.astype(f32) + ssm_dt_bias.astype(f32))
    a = -jnp.exp(ssm_A_log.astype(f32))
    y = ssd_chunked_scan(sx, dt, a, sb.reshape(bsz, s, SSM_GROUPS, SSM_STATE),
                         sc.reshape(bsz, s, SSM_GROUPS, SSM_STATE))
    y = y + ssm_D.astype(f32)[:, None] * sx.astype(f32)
    y = y.reshape(bsz, s, SSM_WIDTH) * jax.nn.silu(sz.astype(f32))
    o_ssm = rms_norm(y.reshape(bsz, s, SSM_GROUPS, SSM_WIDTH // SSM_GROUPS),
                     ssm_out_norm.reshape(SSM_GROUPS, SSM_WIDTH // SSM_GROUPS))
    o_ssm = o_ssm.reshape(bsz, s, SSM_WIDTH)

    mixed = jnp.concatenate([o_fox, o_gdn, o_ssm], axis=-1).astype(h.dtype)
    return mixed @ w_out


def swiglu(h, w_gate, w_up, w_down):
    return (jax.nn.silu(h @ w_gate) * (h @ w_up)) @ w_down


def moe_swiglu(h, router, w_gate, w_up, w_down):
    bsz, s, d = h.shape
    tokens = h.reshape(-1, d)
    n = tokens.shape[0]
    logits = (tokens @ router).astype(jnp.float32)
    top_logits, top_idx = lax.top_k(logits, TOP_K)
    top_w = jax.nn.softmax(top_logits, axis=-1)
    n_assign = n * TOP_K
    flat_e = top_idx.reshape(-1).astype(jnp.int32)
    flat_tok = jnp.arange(n_assign, dtype=jnp.int32) // TOP_K
    flat_w = top_w.reshape(-1)
    order = jnp.argsort(flat_e)
    sorted_e, sorted_tok, sorted_w = flat_e[order], flat_tok[order], flat_w[order]
    counts = jnp.bincount(flat_e, length=N_EXPERTS)
    padded = (counts + MOE_BLOCK - 1) // MOE_BLOCK * MOE_BLOCK
    seg_start = jnp.cumsum(counts) - counts
    pad_end = jnp.cumsum(padded)
    pad_start = pad_end - padded
    dest = pad_start[sorted_e] + jnp.arange(n_assign, dtype=jnp.int32) - seg_start[sorted_e]
    n_blocks = -(-n_assign // MOE_BLOCK) + N_EXPERTS
    slot_tok = jnp.zeros((n_blocks * MOE_BLOCK,), jnp.int32).at[dest].set(sorted_tok)
    slot_w = jnp.zeros((n_blocks * MOE_BLOCK,), jnp.float32).at[dest].set(sorted_w)
    block_start = jnp.arange(n_blocks, dtype=pad_end.dtype) * MOE_BLOCK
    block_expert = jnp.minimum(jnp.searchsorted(pad_end, block_start, side='right'),
                               N_EXPERTS - 1)

    def expert_block(args):
        tok, e = args
        return swiglu(tokens[tok], w_gate[e], w_up[e], w_down[e])

    y = lax.map(expert_block, (slot_tok.reshape(n_blocks, MOE_BLOCK), block_expert))
    y = y.reshape(-1, d) * slot_w[:, None].astype(y.dtype)
    out = jnp.zeros_like(tokens).at[slot_tok].add(y)
    return out.reshape(bsz, s, d)


def setup_inputs(seed: int = 0) -> dict:
    key = jax.random.key(seed)
    ks = jax.random.split(key, 25)
    f32 = jnp.float32

    def normal(k, shape, scale):
        return jax.random.normal(k, shape, f32) * scale

    def gain(k, shape):
        return 1.0 + normal(k, shape, 0.02)

    def dt_bias(k, shape):
        dt = jnp.exp(jax.random.uniform(k, shape, f32, math.log(1e-3), math.log(1e-1)))
        return dt + jnp.log(-jnp.expm1(-dt))

    def a_log(k, shape):
        return jnp.log(jax.random.uniform(k, shape, f32, 1.0, 16.0))

    return {
        'x': normal(ks[0], (BATCH, SEQ, D_MODEL), 1.0),
        'norm_mix': gain(ks[1], (DEPTH, D_MODEL)),
        'w_in': normal(ks[2], (DEPTH, D_MODEL, D_IN_PROJ), D_MODEL ** -0.5),
        'fox_f_bias': FORGET_BIAS_INIT + normal(ks[3], (DEPTH, FOX_HEADS), 0.5),
        'fox_out_norm': gain(ks[4], (DEPTH, FOX_HEADS, FOX_HEAD_DIM)),
        'gdn_conv_w': normal(ks[5], (DEPTH, CONV_WIDTH, GDN_CONV_CH), CONV_WIDTH ** -0.5),
        'gdn_A_log': a_log(ks[6], (DEPTH, GDN_HEADS)),
        'gdn_dt_bias': dt_bias(ks[7], (DEPTH, GDN_HEADS)),
        'gdn_out_norm': gain(ks[8], (DEPTH, GDN_HEAD_DIM)),
        'ssm_conv_w': normal(ks[9], (DEPTH, CONV_WIDTH, SSM_CONV_CH), CONV_WIDTH ** -0.5),
        'ssm_conv_b': normal(ks[10], (DEPTH, SSM_CONV_CH), 0.02),
        'ssm_A_log': a_log(ks[11], (DEPTH, SSM_HEADS)),
        'ssm_dt_bias': dt_bias(ks[12], (DEPTH, SSM_HEADS)),
        'ssm_D': 1.0 + normal(ks[13], (DEPTH, SSM_HEADS), 0.1),
        'ssm_out_norm': gain(ks[14], (DEPTH, SSM_WIDTH)),
        'w_out': normal(ks[15], (DEPTH, D_MIX, D_MODEL), D_MIX ** -0.5),
        'norm_ffn': gain(ks[16], (DEPTH, D_MODEL)),
        'ffn_w_gate': normal(ks[17], (N_DENSE, D_MODEL, FFN_DIM), D_MODEL ** -0.5),
        'ffn_w_up': normal(ks[18], (N_DENSE, D_MODEL, FFN_DIM), D_MODEL ** -0.5),
        'ffn_w_down': normal(ks[19], (N_DENSE, FFN_DIM, D_MODEL), FFN_DIM ** -0.5),
        'moe_router': normal(ks[20], (N_MOE, D_MODEL, N_EXPERTS), D_MODEL ** -0.5),
        'moe_w_gate': normal(ks[21], (N_MOE, N_EXPERTS, D_MODEL, FFN_DIM), D_MODEL ** -0.5),
        'moe_w_up': normal(ks[22], (N_MOE, N_EXPERTS, D_MODEL, FFN_DIM), D_MODEL ** -0.5),
        'moe_w_down': normal(ks[23], (N_MOE, N_EXPERTS, FFN_DIM, D_MODEL), FFN_DIM ** -0.5),
        'norm_final': gain(ks[24], (D_MODEL,)),
    }


def reference(x, norm_mix, w_in, fox_f_bias, fox_out_norm, gdn_conv_w, gdn_A_log, gdn_dt_bias,
              gdn_out_norm, ssm_conv_w, ssm_conv_b, ssm_A_log, ssm_dt_bias, ssm_D, ssm_out_norm,
              w_out, norm_ffn, ffn_w_gate, ffn_w_up, ffn_w_down, moe_router, moe_w_gate,
              moe_w_up, moe_w_down, norm_final):
    for layer in range(DEPTH):
        h = rms_norm(x, norm_mix[layer])
        x = x + hybrid_mixer(h, w_in[layer], fox_f_bias[layer], fox_out_norm[layer],
                             gdn_conv_w[layer], gdn_A_log[layer], gdn_dt_bias[layer],
                             gdn_out_norm[layer], ssm_conv_w[layer], ssm_conv_b[layer],
                             ssm_A_log[layer], ssm_dt_bias[layer], ssm_D[layer],
                             ssm_out_norm[layer], w_out[layer])
        h = rms_norm(x, norm_ffn[layer])
        j = layer // 2
        if layer % 2 == 0:
            x = x + swiglu(h, ffn_w_gate[j], ffn_w_up[j], ffn_w_down[j])
        else:
            x = x + moe_swiglu(h, moe_router[j], moe_w_gate[j], moe_w_up[j], moe_w_down[j])
    return rms_norm(x, norm_final)
```

```python
import functools

import jax
import jax.numpy as jnp
from jax import lax
from jax.experimental import pallas as pl
from jax.experimental.pallas import tpu as pltpu

F32 = jnp.float32
BF16 = jnp.bfloat16
I32 = jnp.int32

NORM_EPS = 1e-6
LANES = 128
NEG_BIG = -1e30

FOX_HEADS = 4
HEAD_DIM = 128
GDN_HEADS = 6
GDN_CHUNK = 64
SSM_HEADS = 12
SSM_HEAD_DIM = 64
SSM_STATE = 128
SSM_GROUPS = 2
SSM_CHUNK = 128
CONV_WIDTH = 4
N_EXPERTS = 8
TOP_K = 2

FOX_W = FOX_HEADS * HEAD_DIM
GDN_W = GDN_HEADS * HEAD_DIM
SSM_W = SSM_HEADS * SSM_HEAD_DIM
BC_W = SSM_GROUPS * SSM_STATE
PROJ_SIZES = (FOX_W, FOX_W, FOX_W, FOX_HEADS, 3 * GDN_W, GDN_W, GDN_HEADS, GDN_HEADS,
              SSM_W, SSM_W + 2 * BC_W, SSM_HEADS)

P_FQ, P_FK, P_FV = 0, FOX_W, 2 * FOX_W
P_GQKV = 3 * FOX_W
P_GZ = P_GQKV + 3 * GDN_W
P_SZ = P_GZ + GDN_W
P_SXBC = P_SZ + SSM_W
P_SMALL = P_SXBC + SSM_W + 2 * BC_W
P_WIDTH = 6912
S_FF, S_GB, S_GA, S_DT = 0, 4, 10, 16
S_END = S_DT + SSM_HEADS
C_GQ, C_GK, C_GV = 0, GDN_W, 2 * GDN_W
C_SX = 3 * GDN_W
C_SB = C_SX + SSM_W
C_SC = C_SB + BC_W
C_WIDTH = C_SC + BC_W

MOE_ROWS = 1024
VMEM_LIMIT = 56 * 1024 * 1024


def _tile(n, pref):
    t = min(n, pref)
    while n % t:
        t //= 2
    return t


def _params(sem, vmem=VMEM_LIMIT):
    return pltpu.CompilerParams(dimension_semantics=sem, vmem_limit_bytes=vmem)


def _silu(x):
    return x / (1.0 + jnp.exp(-x))


def _softplus(x):
    return jnp.maximum(x, 0.0) + jnp.log1p(jnp.exp(-jnp.abs(x)))


def _rms(x, gain):
    return x * lax.rsqrt(jnp.mean(x * x, axis=-1, keepdims=True) + NORM_EPS) * gain


def _dot(a, b):
    return jnp.dot(a, b, preferred_element_type=F32)


def _dot_nt(a, b):
    return lax.dot_general(a, b, (((1,), (1,)), ((), ())), preferred_element_type=F32)


def _dot_tn(a, b):
    return lax.dot_general(a, b, (((0,), (0,)), ((), ())), preferred_element_type=F32)


def _norm_matmul_kernel(x_ref, g_ref, w_ref, o_ref, h_ref):
    @pl.when(pl.program_id(1) == 0)
    def _():
        h_ref[...] = _rms(x_ref[...], g_ref[...]).astype(BF16)

    o_ref[...] = _dot(h_ref[...], w_ref[...])


def _norm_matmul(x, gain, w):
    n, d = x.shape
    nout = w.shape[1]
    tm = _tile(n, 1024)
    tn = _tile(nout, 768)
    return pl.pallas_call(
        _norm_matmul_kernel,
        out_shape=jax.ShapeDtypeStruct((n, nout), F32),
        grid=(n // tm, nout // tn),
        in_specs=[pl.BlockSpec((tm, d), lambda i, j: (i, 0)),
                  pl.BlockSpec((1, d), lambda i, j: (0, 0)),
                  pl.BlockSpec((d, tn), lambda i, j: (0, j))],
        out_specs=pl.BlockSpec((tm, tn), lambda i, j: (i, j)),
        scratch_shapes=[pltpu.VMEM((tm, d), BF16)],
        compiler_params=_params(("parallel", "arbitrary")),
        name="norm_inproj",
    )(x, gain, w)


def _prep_kernel(p_ref, par_ref, val_ref, cs_ref, cst_ref, carry_ref):
    @pl.when(pl.program_id(1) == 0)
    def _():
        carry_ref[...] = jnp.zeros_like(carry_ref)

    blk = p_ref.shape[1]
    v = p_ref[0] + par_ref[0:1, :]
    lane = lax.broadcasted_iota(I32, v.shape, 1)
    sp = _softplus(v)
    log_f = -_softplus(-v)
    beta = 1.0 / (1.0 + jnp.exp(-v))
    neg_a = -jnp.exp(par_ref[1:2, :])
    val_ref[0] = jnp.where(lane < S_GA, beta, sp)
    z = jnp.where(lane < S_GB, log_f, jnp.where(lane < S_GA, 0.0, neg_a * sp))
    z = jnp.where(lane < S_END, z, 0.0)
    row = lax.broadcasted_iota(I32, (blk, blk), 0)
    col = lax.broadcasted_iota(I32, (blk, blk), 1)
    tri = row >= col
    tri_blk = jnp.where(tri, 1.0, 0.0)
    tri_gdn = jnp.where(tri & (row // GDN_CHUNK == col // GDN_CHUNK), 1.0, 0.0)
    cs_blk = jnp.dot(tri_blk, z, precision=lax.Precision.HIGHEST, preferred_element_type=F32)
    cs_gdn = jnp.dot(tri_gdn, z, precision=lax.Precision.HIGHEST, preferred_element_type=F32)
    cs_run = cs_blk + carry_ref[...]
    carry_ref[...] = cs_run[blk - 1:blk, :]
    cs = jnp.where(lane < S_GB, cs_run, jnp.where(lane < S_DT, cs_gdn, cs_blk))
    cs_ref[0] = cs
    cst_ref[0] = cs.T


def _prep(p3, par):
    b, s, _ = p3.shape
    blk = SSM_CHUNK
    shp = jax.ShapeDtypeStruct((b, s, LANES), F32)
    return pl.pallas_call(
        _prep_kernel,
        out_shape=(shp, shp, jax.ShapeDtypeStruct((b, LANES, s), F32)),
        grid=(b, s // blk),
        in_specs=[pl.BlockSpec((1, blk, LANES), lambda bi, i: (bi, i, P_SMALL // LANES)),
                  pl.BlockSpec((8, LANES), lambda bi, i: (0, 0))],
        out_specs=(pl.BlockSpec((1, blk, LANES), lambda bi, i: (bi, i, 0)),
                   pl.BlockSpec((1, blk, LANES), lambda bi, i: (bi, i, 0)),
                   pl.BlockSpec((1, LANES, blk), lambda bi, i: (bi, 0, i))),
        scratch_shapes=[pltpu.VMEM((1, LANES), F32)],
        compiler_params=_params(("parallel", "arbitrary")),
        name="gate_prep",
    )(p3, par)


CONV_COLS = 256
CONV_GDN_BLOCKS = 3 * GDN_W // CONV_COLS
CONV_L2_BLOCKS = 2 * GDN_W // CONV_COLS
CONV_Q_BLOCKS = GDN_W // CONV_COLS


def _conv_kernel(u_ref, halo_ref, w_ref, b_ref, o_ref):
    i = pl.program_id(1)
    c = pl.program_id(2)
    u = u_ref[0]
    t = u.shape[0]
    halo = jnp.where(i > 0, halo_ref[0], 0.0)
    row8 = lax.broadcasted_iota(I32, halo.shape, 0)
    acc = u * w_ref[CONV_WIDTH - 1:CONV_WIDTH, :]
    head = u[0:8] * w_ref[CONV_WIDTH - 1:CONV_WIDTH, :]
    for s in range(1, CONV_WIDTH):
        wk = w_ref[CONV_WIDTH - 1 - s:CONV_WIDTH - s, :]
        rolled = pltpu.roll(u, s, 0)
        acc = acc + rolled * wk
        fixed = jnp.where(row8 < s, pltpu.roll(halo, s, 0), rolled[0:8])
        head = head + fixed * wk
    y = jnp.concatenate([head, acc[8:]], axis=0) if t > 8 else head
    y = _silu(y + b_ref[...])
    parts = []
    for k in range(CONV_COLS // HEAD_DIM):
        yk = y[:, k * HEAD_DIM:(k + 1) * HEAD_DIM]
        nk = yk * lax.rsqrt(jnp.sum(yk * yk, axis=-1, keepdims=True) + NORM_EPS)
        nk = nk * jnp.where(c < CONV_Q_BLOCKS, HEAD_DIM ** -0.5, 1.0)
        parts.append(jnp.where(c < CONV_L2_BLOCKS, nk, yk))
    o_ref[0] = jnp.concatenate(parts, axis=1)


def _conv(p3, cw, cb):
    b, s, _ = p3.shape
    t = _tile(s, 1024)
    gdn0 = P_GQKV // CONV_COLS
    ssm_shift = P_SXBC // CONV_COLS - CONV_GDN_BLOCKS

    def col(c):
        return jnp.where(c < CONV_GDN_BLOCKS, c + gdn0, c + ssm_shift)

    return pl.pallas_call(
        _conv_kernel,
        out_shape=jax.ShapeDtypeStruct((b, s, C_WIDTH), F32),
        grid=(b, s // t, C_WIDTH // CONV_COLS),
        in_specs=[pl.BlockSpec((1, t, CONV_COLS), lambda bi, i, c: (bi, i, col(c))),
                  pl.BlockSpec((1, 8, CONV_COLS),
                               lambda bi, i, c: (bi, jnp.maximum(i * (t // 8) - 1, 0), col(c))),
                  pl.BlockSpec((CONV_WIDTH, CONV_COLS), lambda bi, i, c: (0, c)),
                  pl.BlockSpec((1, CONV_COLS), lambda bi, i, c: (0, c))],
        out_specs=pl.BlockSpec((1, t, CONV_COLS), lambda bi, i, c: (bi, i, c)),
        compiler_params=_params(("parallel", "parallel", "parallel")),
        name="conv_silu",
    )(p3, p3, cw, cb)


def _fox_kernel(q_ref, k_ref, v_ref, cq_ref, ck_ref, gn_ref, o_ref, m_ref, l_ref, acc_ref):
    i = pl.program_id(1)
    j = pl.program_id(2)
    tq = q_ref.shape[1]
    tk = k_ref.shape[1]

    @pl.when(j == 0)
    def _():
        m_ref[...] = jnp.full_like(m_ref, NEG_BIG)
        l_ref[...] = jnp.zeros_like(l_ref)
        acc_ref[...] = jnp.zeros_like(acc_ref)

    @pl.when(j <= i)
    def _():
        qpos = i * tq + lax.broadcasted_iota(I32, (tq, tk), 0)
        kpos = j * tk + lax.broadcasted_iota(I32, (tq, tk), 1)
        causal = kpos <= qpos
        for h in range(FOX_HEADS):
            hs = slice(h * HEAD_DIM, (h + 1) * HEAD_DIM)
            q = (q_ref[0, :, hs] * HEAD_DIM ** -0.5).astype(BF16)
            k = k_ref[0, :, hs].astype(BF16)
            v = v_ref[0, :, hs].astype(BF16)
            s = _dot_nt(q, k) + cq_ref[0, :, h:h + 1] - ck_ref[0, h:h + 1, :]
            s = jnp.where(causal, s, NEG_BIG)
            m_prev = m_ref[h]
            m_new = jnp.maximum(m_prev, jnp.max(s, axis=-1, keepdims=True))
            alpha = jnp.exp(m_prev - m_new)
            p = jnp.exp(s - m_new)
            l_ref[h] = alpha * l_ref[h] + jnp.sum(p, axis=-1, keepdims=True)
            acc_ref[:, hs] = alpha * acc_ref[:, hs] + _dot(p.astype(BF16), v)
            m_ref[h] = m_new

    @pl.when(j == pl.num_programs(2) - 1)
    def _():
        for h in range(FOX_HEADS):
            hs = slice(h * HEAD_DIM, (h + 1) * HEAD_DIM)
            o = acc_ref[:, hs] / l_ref[h]
            o_ref[0, :, hs] = _rms(o, gn_ref[:, hs]).astype(o_ref.dtype)


def _fox(p3, cs, cst, gain):
    b, s, _ = p3.shape
    t = _tile(s, 512)
    n = s // t
    return pl.pallas_call(
        _fox_kernel,
        out_shape=jax.ShapeDtypeStruct((b, s, FOX_W), BF16),
        grid=(b, n, n),
        in_specs=[pl.BlockSpec((1, t, FOX_W), lambda bi, i, j: (bi, i, P_FQ // FOX_W)),
                  pl.BlockSpec((1, t, FOX_W), lambda bi, i, j: (bi, jnp.minimum(i, j), P_FK // FOX_W)),
                  pl.BlockSpec((1, t, FOX_W), lambda bi, i, j: (bi, jnp.minimum(i, j), P_FV // FOX_W)),
                  pl.BlockSpec((1, t, LANES), lambda bi, i, j: (bi, i, 0)),
                  pl.BlockSpec((1, 8, t), lambda bi, i, j: (bi, 0, jnp.minimum(i, j))),
                  pl.BlockSpec((1, FOX_W), lambda bi, i, j: (0, 0))],
        out_specs=pl.BlockSpec((1, t, FOX_W), lambda bi, i, j: (bi, i, 0)),
        scratch_shapes=[pltpu.VMEM((FOX_HEADS, t, 1), F32),
                        pltpu.VMEM((FOX_HEADS, t, 1), F32),
                        pltpu.VMEM((t, FOX_W), F32)],
        compiler_params=_params(("parallel", "parallel", "arbitrary")),
        name="fox_attention",
    )(p3, p3, p3, cs, cst, gain)


def _gdn_kernel(q_ref, k_ref, v_ref, z_ref, val_ref, cs_ref, cst_ref, gn_ref, o_ref, state_ref):
    @pl.when(pl.program_id(1) == 0)
    def _():
        state_ref[...] = jnp.zeros_like(state_ref)

    c = GDN_CHUNK
    row = lax.broadcasted_iota(I32, (c, c), 0)
    col = lax.broadcasted_iota(I32, (c, c), 1)
    incl = row >= col
    strict = row > col
    eye = jnp.where(row == col, 1.0, 0.0)
    for h in range(GDN_HEADS):
        hs = slice(h * HEAD_DIM, (h + 1) * HEAD_DIM)
        state = state_ref[h]
        for ci in range(q_ref.shape[1] // c):
            rs = slice(ci * c, (ci + 1) * c)
            q = q_ref[0, rs, hs]
            k = k_ref[0, rs, hs]
            v = v_ref[0, rs, hs]
            beta = val_ref[0, rs, S_GB + h:S_GB + h + 1]
            gc = cs_ref[0, rs, S_GA + h:S_GA + h + 1]
            gr = cst_ref[0, S_GA + h:S_GA + h + 1, rs]
            decay = jnp.where(incl, jnp.exp(jnp.where(incl, gc - gr, 0.0)), 0.0)
            eg = jnp.exp(gc)
            g_last = gc[c - 1:c, :]
            kb = k * beta
            a = jnp.where(strict, _dot_nt(kb, k) * decay, 0.0)
            tinv = eye
            sz = 1
            while sz < c:
                off = (row // (2 * sz) == col // (2 * sz)) & (row // sz != col // sz)
                a_off = jnp.where(off, a, 0.0)
                tinv = tinv - _dot(_dot(tinv, a_off), tinv) if sz > 1 else eye - a_off
                sz *= 2
            sol = _dot(tinv, jnp.concatenate([v * beta, kb * eg], axis=1))
            u = sol[:, :HEAD_DIM]
            w = sol[:, HEAD_DIM:]
            qk = jnp.where(incl, _dot_nt(q, k) * decay, 0.0)
            v_new = u - _dot(w, state)
            o = _dot(q * eg, state) + _dot(qk, v_new)
            state = state * jnp.exp(g_last) + _dot_tn(k * jnp.exp(g_last - gc), v_new)
            o = _rms(o, gn_ref[...]) * _silu(z_ref[0, rs, hs])
            o_ref[0, rs, hs] = o.astype(o_ref.dtype)
        state_ref[h] = state


def _gdn(cv, p3, val, cs, cst, gain):
    b, s, _ = cv.shape
    t = SSM_CHUNK
    wide = lambda blk: pl.BlockSpec((1, t, GDN_W), lambda bi, i: (bi, i, blk))
    small = pl.BlockSpec((1, t, LANES), lambda bi, i: (bi, i, 0))
    return pl.pallas_call(
        _gdn_kernel,
        out_shape=jax.ShapeDtypeStruct((b, s, GDN_W), BF16),
        grid=(b, s // t),
        in_specs=[wide(C_GQ // GDN_W), wide(C_GK // GDN_W), wide(C_GV // GDN_W),
                  wide(P_GZ // GDN_W), small, small,
                  pl.BlockSpec((1, LANES, t), lambda bi, i: (bi, 0, i)),
                  pl.BlockSpec((1, HEAD_DIM), lambda bi, i: (0, 0))],
        out_specs=pl.BlockSpec((1, t, GDN_W), lambda bi, i: (bi, i, 0)),
        scratch_shapes=[pltpu.VMEM((GDN_HEADS, HEAD_DIM, HEAD_DIM), F32)],
        compiler_params=_params(("parallel", "arbitrary")),
        name="gated_deltanet",
    )(cv, cv, cv, p3, val, cs, cst, gain)


SSM_PAIRS = SSM_HEADS // 2
PAIRS_PER_GROUP = SSM_PAIRS // SSM_GROUPS


def _ssd_kernel(x_ref, b_ref, c_ref, z_ref, val_ref, cs_ref, cst_ref, par_ref, o_ref, st_ref):
    @pl.when(pl.program_id(1) == 0)
    def _():
        st_ref[...] = jnp.zeros_like(st_ref)

    n = x_ref.shape[1]
    row = lax.broadcasted_iota(I32, (n, n), 0)
    col = lax.broadcasted_iota(I32, (n, n), 1)
    incl = row >= col
    lo = lax.broadcasted_iota(I32, (n, LANES), 1) < SSM_HEAD_DIM
    lo_row = lo[0:1, :]

    def decay_of(h):
        ac = cs_ref[0, :, S_DT + h:S_DT + h + 1]
        ar = cst_ref[0, S_DT + h:S_DT + h + 1, :]
        return jnp.where(incl, jnp.exp(jnp.where(incl, ac - ar, 0.0)), 0.0), ac

    for g in range(SSM_GROUPS):
        bg = b_ref[0, :, g * SSM_STATE:(g + 1) * SSM_STATE]
        cg = c_ref[0, :, g * SSM_STATE:(g + 1) * SSM_STATE]
        cb = _dot_nt(cg, bg)
        bg_t = bg.T
        ys = []
        ssq = jnp.zeros((n, 1), F32)
        for pp in range(PAIRS_PER_GROUP):
            p = g * PAIRS_PER_GROUP + pp
            ps = slice(p * LANES, (p + 1) * LANES)
            h0, h1 = 2 * p, 2 * p + 1
            xp = x_ref[0, :, ps]
            dt = jnp.where(lo, val_ref[0, :, S_DT + h0:S_DT + h0 + 1],
                           val_ref[0, :, S_DT + h1:S_DT + h1 + 1])
            xdt = xp * dt
            d0, ac0 = decay_of(h0)
            d1, ac1 = decay_of(h1)
            y_diag = jnp.where(lo, _dot(cb * d0, xdt), _dot(cb * d1, xdt))
            st = st_ref[p]
            y_off = _dot(cg, st) * jnp.where(lo, jnp.exp(ac0), jnp.exp(ac1))
            al0 = ac0[n - 1:n, :]
            al1 = ac1[n - 1:n, :]
            to_end = jnp.where(lo, jnp.exp(al0 - ac0), jnp.exp(al1 - ac1))
            st_ref[p] = st * jnp.where(lo_row, jnp.exp(al0), jnp.exp(al1)) + _dot(bg_t, to_end * xdt)
            y = (y_diag + y_off + par_ref[0:1, ps] * xp) * _silu(z_ref[0, :, ps])
            ys.append(y)
            ssq = ssq + jnp.sum(y * y, axis=-1, keepdims=True)
        inv = lax.rsqrt(ssq / (PAIRS_PER_GROUP * LANES) + NORM_EPS)
        for pp in range(PAIRS_PER_GROUP):
            ps = slice((g * PAIRS_PER_GROUP + pp) * LANES, (g * PAIRS_PER_GROUP + pp + 1) * LANES)
            o_ref[0, :, ps] = (ys[pp] * inv * par_ref[1:2, ps]).astype(o_ref.dtype)


def _ssd(cv, p3, val, cs, cst, par):
    b, s, _ = cv.shape
    t = SSM_CHUNK
    small = pl.BlockSpec((1, t, LANES), lambda bi, i: (bi, i, 0))
    return pl.pallas_call(
        _ssd_kernel,
        out_shape=jax.ShapeDtypeStruct((b, s, SSM_W), BF16),
        grid=(b, s // t),
        in_specs=[pl.BlockSpec((1, t, SSM_W), lambda bi, i: (bi, i, C_SX // SSM_W)),
                  pl.BlockSpec((1, t, BC_W), lambda bi, i: (bi, i, C_SB // BC_W)),
                  pl.BlockSpec((1, t, BC_W), lambda bi, i: (bi, i, C_SC // BC_W)),
                  pl.BlockSpec((1, t, SSM_W), lambda bi, i: (bi, i, P_SZ // SSM_W)),
                  small, small,
                  pl.BlockSpec((1, LANES, t), lambda bi, i: (bi, 0, i)),
                  pl.BlockSpec((8, SSM_W), lambda bi, i: (0, 0))],
        out_specs=pl.BlockSpec((1, t, SSM_W), lambda bi, i: (bi, i, 0)),
        scratch_shapes=[pltpu.VMEM((SSM_PAIRS, SSM_STATE, LANES), F32)],
        compiler_params=_params(("parallel", "arbitrary")),
        name="ssd_scan",
    )(cv, cv, cv, p3, val, cs, cst, par)


def _outproj_kernel(a1_ref, a2_ref, a3_ref, w1_ref, w2_ref, w3_ref, x_ref, o_ref):
    o_ref[...] = (x_ref[...] + _dot(a1_ref[...], w1_ref[...]) + _dot(a2_ref[...], w2_ref[...])
                  + _dot(a3_ref[...], w3_ref[...]))


def _outproj(a1, a2, a3, w1, w2, w3, x):
    n, d = x.shape
    tm = _tile(n, 1024)
    tn = _tile(d, 512)
    act = lambda a: pl.BlockSpec((tm, a.shape[1]), lambda i, j: (i, 0))
    wt = lambda w: pl.BlockSpec((w.shape[0], tn), lambda i, j: (0, j))
    return pl.pallas_call(
        _outproj_kernel,
        out_shape=jax.ShapeDtypeStruct((n, d), F32),
        grid=(n // tm, d // tn),
        in_specs=[act(a1), act(a2), act(a3), wt(w1), wt(w2), wt(w3),
                  pl.BlockSpec((tm, tn), lambda i, j: (i, j))],
        out_specs=pl.BlockSpec((tm, tn), lambda i, j: (i, j)),
        compiler_params=_params(("parallel", "parallel")),
        name="out_proj",
    )(a1, a2, a3, w1, w2, w3, x)


def _ffn_kernel(x_ref, g_ref, wg_ref, wu_ref, wd_ref, o_ref, h_ref, acc_ref):
    f = pl.program_id(1)

    @pl.when(f == 0)
    def _():
        h_ref[...] = _rms(x_ref[...], g_ref[...]).astype(BF16)
        acc_ref[...] = jnp.zeros_like(acc_ref)

    h = h_ref[...]
    act = _silu(_dot(h, wg_ref[...])) * _dot(h, wu_ref[...])
    acc_ref[...] += _dot(act.astype(BF16), wd_ref[...])

    @pl.when(f == pl.num_programs(1) - 1)
    def _():
        o_ref[...] = x_ref[...] + acc_ref[...]


def _ffn(x, gain, wg, wu, wd):
    n, d = x.shape
    f = wg.shape[1]
    tm = _tile(n, 512)
    tf = _tile(f, 512)
    return pl.pallas_call(
        _ffn_kernel,
        out_shape=jax.ShapeDtypeStruct((n, d), F32),
        grid=(n // tm, f // tf),
        in_specs=[pl.BlockSpec((tm, d), lambda i, j: (i, 0)),
                  pl.BlockSpec((1, d), lambda i, j: (0, 0)),
                  pl.BlockSpec((d, tf), lambda i, j: (0, j)),
                  pl.BlockSpec((d, tf), lambda i, j: (0, j)),
                  pl.BlockSpec((tf, d), lambda i, j: (j, 0))],
        out_specs=pl.BlockSpec((tm, d), lambda i, j: (i, 0)),
        scratch_shapes=[pltpu.VMEM((tm, d), BF16), pltpu.VMEM((tm, d), F32)],
        compiler_params=_params(("parallel", "arbitrary")),
        name="dense_ffn",
    )(x, gain, wg, wu, wd)


def _router_kernel(x_ref, g_ref, r_ref, idx_ref, w_ref):
    h = _rms(x_ref[...], g_ref[...])
    logits = jnp.dot(h, r_ref[...], precision=lax.Precision.HIGHEST, preferred_element_type=F32)
    lane = lax.broadcasted_iota(I32, logits.shape, 1)
    l1 = jnp.where(lane < N_EXPERTS, logits, NEG_BIG)
    m1 = jnp.max(l1, axis=-1, keepdims=True)
    i1 = jnp.min(jnp.where(l1 == m1, lane, LANES), axis=-1, keepdims=True)
    l2 = jnp.where(lane == i1, NEG_BIG, l1)
    m2 = jnp.max(l2, axis=-1, keepdims=True)
    i2 = jnp.min(jnp.where(l2 == m2, lane, LANES), axis=-1, keepdims=True)
    e = jnp.exp(m2 - m1)
    idx_ref[...] = jnp.where(lane == 0, i1, jnp.where(lane == 1, i2, 0))
    w_ref[...] = jnp.where(lane == 0, 1.0 / (1.0 + e), jnp.where(lane == 1, e / (1.0 + e), 0.0))


def _router(x, gain, router_pad):
    n, d = x.shape
    tm = _tile(n, 512)
    return pl.pallas_call(
        _router_kernel,
        out_shape=(jax.ShapeDtypeStruct((n, LANES), I32), jax.ShapeDtypeStruct((n, LANES), F32)),
        grid=(n // tm,),
        in_specs=[pl.BlockSpec((tm, d), lambda i: (i, 0)),
                  pl.BlockSpec((1, d), lambda i: (0, 0)),
                  pl.BlockSpec((d, LANES), lambda i: (0, 0))],
        out_specs=(pl.BlockSpec((tm, LANES), lambda i: (i, 0)),
                   pl.BlockSpec((tm, LANES), lambda i: (i, 0))),
        compiler_params=_params(("parallel",)),
        name="moe_router",
    )(x, gain, router_pad)


def _row_copy(src_hbm, src_row, dst, dst_row, sem):
    return pltpu.make_async_copy(src_hbm.at[pl.ds(src_row, 1)], dst.at[pl.ds(dst_row, 1)], sem)


def _gather_norm_kernel(tok_ref, x_hbm, g_ref, o_ref, buf, sem):
    rows = o_ref.shape[0]

    def start(r, carry):
        _row_copy(x_hbm, tok_ref[0, 0, r], buf, r, sem).start()
        return carry

    def wait(r, carry):
        _row_copy(x_hbm, 0, buf, r, sem).wait()
        return carry

    lax.fori_loop(0, rows, start, 0)
    lax.fori_loop(0, rows, wait, 0)
    o_ref[...] = _rms(buf[...], g_ref[...]).astype(o_ref.dtype)


def _gather_norm(x, gain, slot_tok):
    n, d = x.shape
    n_slots = slot_tok.shape[0]
    rb = 256
    return pl.pallas_call(
        _gather_norm_kernel,
        out_shape=jax.ShapeDtypeStruct((n_slots, d), BF16),
        grid=(n_slots // rb,),
        in_specs=[pl.BlockSpec((1, 1, rb), lambda i: (i, 0, 0), memory_space=pltpu.SMEM),
                  pl.BlockSpec(memory_space=pl.ANY),
                  pl.BlockSpec((1, d), lambda i: (0, 0))],
        out_specs=pl.BlockSpec((rb, d), lambda i: (i, 0)),
        scratch_shapes=[pltpu.VMEM((rb, d), F32), pltpu.SemaphoreType.DMA(())],
        compiler_params=_params(("arbitrary",)),
        name="moe_gather",
    )(slot_tok.reshape(n_slots // rb, 1, rb), x, gain)


def _expert_ffn_kernel(e_ref, na_ref, x_ref, wg_ref, wu_ref, wd_ref, o_ref):
    s = pl.program_id(0)
    f = pl.program_id(1)
    active = s < na_ref[0]

    @pl.when(f == 0)
    def _():
        o_ref[...] = jnp.zeros_like(o_ref)

    @pl.when(active)
    def _():
        x = x_ref[...]
        gate = _dot(x, wg_ref[0].astype(BF16))
        up = _dot(x, wu_ref[0].astype(BF16))
        o_ref[...] += _dot((_silu(gate) * up).astype(BF16), wd_ref[0].astype(BF16))


def _expert_ffn(xs, blk_expert, n_active, wg, wu, wd):
    n_slots, d = xs.shape
    f = wg.shape[2]
    nblk = n_slots // MOE_ROWS
    tf = _tile(f, 256)
    nf = f // tf

    def fidx(s, j, na_ref):
        return jnp.where(s < na_ref[0], j, nf - 1)

    grid_spec = pltpu.PrefetchScalarGridSpec(
        num_scalar_prefetch=2,
        grid=(nblk, nf),
        in_specs=[pl.BlockSpec((MOE_ROWS, d), lambda s, j, e, na: (s, 0)),
                  pl.BlockSpec((1, d, tf), lambda s, j, e, na: (e[s], 0, fidx(s, j, na))),
                  pl.BlockSpec((1, d, tf), lambda s, j, e, na: (e[s], 0, fidx(s, j, na))),
                  pl.BlockSpec((1, tf, d), lambda s, j, e, na: (e[s], fidx(s, j, na), 0))],
        out_specs=pl.BlockSpec((MOE_ROWS, d), lambda s, j, e, na: (s, 0)),
    )
    return pl.pallas_call(
        _expert_ffn_kernel,
        out_shape=jax.ShapeDtypeStruct((n_slots, d), F32),
        grid_spec=grid_spec,
        compiler_params=_params(("arbitrary", "arbitrary")),
        name="moe_expert_ffn",
    )(blk_expert, n_active, xs, wg, wu, wd)


def _combine_kernel(d0_ref, d1_ref, x_ref, w_ref, y_hbm, o_ref, buf, sem):
    rows = o_ref.shape[0]

    def start(r, carry):
        _row_copy(y_hbm, d0_ref[0, 0, r], buf.at[0], r, sem).start()
        _row_copy(y_hbm, d1_ref[0, 0, r], buf.at[1], r, sem).start()
        return carry

    def wait(r, carry):
        _row_copy(y_hbm, 0, buf.at[0], r, sem).wait()
        _row_copy(y_hbm, 0, buf.at[1], r, sem).wait()
        return carry

    lax.fori_loop(0, rows, start, 0)
    lax.fori_loop(0, rows, wait, 0)
    o_ref[...] = x_ref[...] + w_ref[:, 0:1] * buf[0] + w_ref[:, 1:2] * buf[1]


def _combine(x, top_w, dest0, dest1, ys):
    n, d = x.shape
    tb = _tile(n, 256)
    dspec = pl.BlockSpec((1, 1, tb), lambda i: (i, 0, 0), memory_space=pltpu.SMEM)
    return pl.pallas_call(
        _combine_kernel,
        out_shape=jax.ShapeDtypeStruct((n, d), F32),
        grid=(n // tb,),
        in_specs=[dspec, dspec,
                  pl.BlockSpec((tb, d), lambda i: (i, 0)),
                  pl.BlockSpec((tb, LANES), lambda i: (i, 0)),
                  pl.BlockSpec(memory_space=pl.ANY)],
        out_specs=pl.BlockSpec((tb, d), lambda i: (i, 0)),
        scratch_shapes=[pltpu.VMEM((2, tb, d), F32), pltpu.SemaphoreType.DMA(())],
        compiler_params=_params(("arbitrary",)),
        name="moe_combine",
    )(dest0.reshape(n // tb, 1, tb), dest1.reshape(n // tb, 1, tb), x, top_w, ys)


def _moe(x, gain, router, wg, wu, wd):
    n, d = x.shape
    router_pad = jnp.pad(router, ((0, 0), (0, LANES - N_EXPERTS)))
    top_idx, top_w = _router(x, gain, router_pad)
    flat_e = top_idx[:, :TOP_K].reshape(-1)
    onehot = (flat_e[:, None] == jnp.arange(N_EXPERTS, dtype=I32)[None, :]).astype(I32)
    rank = jnp.take_along_axis(jnp.cumsum(onehot, axis=0) - onehot, flat_e[:, None], axis=1)[:, 0]
    counts = jnp.sum(onehot, axis=0)
    padded = (counts + MOE_ROWS - 1) // MOE_ROWS * MOE_ROWS
    pad_end = jnp.cumsum(padded)
    dest = (pad_end - padded)[flat_e] + rank
    nblk = (n * TOP_K) // MOE_ROWS + N_EXPERTS
    n_slots = nblk * MOE_ROWS
    slot_tok = jnp.zeros((n_slots,), I32).at[dest].set(jnp.arange(n * TOP_K, dtype=I32) // TOP_K)
    n_active = (pad_end[-1] // MOE_ROWS).astype(I32)
    blk_start = jnp.arange(nblk, dtype=I32) * MOE_ROWS
    blk_expert = jnp.minimum(jnp.searchsorted(pad_end, blk_start, side='right'), N_EXPERTS - 1)
    last_e = blk_expert[jnp.maximum(n_active - 1, 0)]
    blk_expert = jnp.where(jnp.arange(nblk) < n_active, blk_expert, last_e).astype(I32)

    xs = _gather_norm(x, gain, slot_tok)
    ys = _expert_ffn(xs, blk_expert, n_active.reshape(1), wg, wu, wd)
    dest2 = dest.reshape(n, TOP_K)
    return _combine(x, top_w, dest2[:, 0], dest2[:, 1], ys)


def _final_norm_kernel(x_ref, g_ref, o_ref):
    o_ref[...] = _rms(x_ref[...], g_ref[...])


def _final_norm(x, gain):
    n, d = x.shape
    tm = _tile(n, 512)
    return pl.pallas_call(
        _final_norm_kernel,
        out_shape=jax.ShapeDtypeStruct((n, d), F32),
        grid=(n // tm,),
        in_specs=[pl.BlockSpec((tm, d), lambda i: (i, 0)), pl.BlockSpec((1, d), lambda i: (0, 0))],
        out_specs=pl.BlockSpec((tm, d), lambda i: (i, 0)),
        compiler_params=_params(("parallel",)),
        name="final_norm",
    )(x, gain)


def _layout_w_in(w_in):
    offs = [0]
    for sz in PROJ_SIZES:
        offs.append(offs[-1] + sz)
    seg = [w_in[..., offs[k]:offs[k + 1]] for k in range(len(PROJ_SIZES))]
    fq, fk, fv, ff, gqkv, gz, gb, ga, sz_, sxbc, sdt = seg
    lead = w_in.shape[:-1]
    small = jnp.concatenate([ff, gb, ga, sdt, jnp.zeros(lead + (LANES - S_END,), w_in.dtype)], -1)
    tail = jnp.zeros(lead + (P_WIDTH - P_SMALL - LANES,), w_in.dtype)
    return jnp.concatenate([fq, fk, fv, gqkv, gz, sz_, sxbc, small, tail], -1).astype(BF16)


def _row_tile(rows, width):
    out = jnp.zeros((8, width), F32)
    for r, v in enumerate(rows):
        out = out.at[r, :v.shape[0]].set(v.astype(F32))
    return out


def kernel(x, norm_mix, w_in, fox_f_bias, fox_out_norm, gdn_conv_w, gdn_A_log, gdn_dt_bias,
           gdn_out_norm, ssm_conv_w, ssm_conv_b, ssm_A_log, ssm_dt_bias, ssm_D, ssm_out_norm,
           w_out, norm_ffn, ffn_w_gate, ffn_w_up, ffn_w_down, moe_router, moe_w_gate,
           moe_w_up, moe_w_down, norm_final):
    b, s, d = x.shape
    depth = w_in.shape[0]
    n = b * s
    w_in_l = _layout_w_in(w_in)
    w_out_b = w_out.astype(BF16)
    ffn_g, ffn_u, ffn_d = (w.astype(BF16) for w in (ffn_w_gate, ffn_w_up, ffn_w_down))
    conv_w = jnp.concatenate([gdn_conv_w, ssm_conv_w], axis=-1)
    conv_b = jnp.concatenate([jnp.zeros((depth, 3 * GDN_W), F32), ssm_conv_b], axis=-1)

    xf = x.reshape(n, d)
    for layer in range(depth):
        p = _norm_matmul(xf, norm_mix[layer][None, :], w_in_l[layer])
        p3 = p.reshape(b, s, P_WIDTH)
        zeros = lambda k: jnp.zeros((k,), F32)
        bias_row = jnp.concatenate([fox_f_bias[layer], zeros(GDN_HEADS), gdn_dt_bias[layer],
                                    ssm_dt_bias[layer]])
        alog_row = jnp.concatenate([zeros(S_GA), gdn_A_log[layer], ssm_A_log[layer]])
        val, cs, cst = _prep(p3, _row_tile([bias_row, alog_row], LANES))
        cv = _conv(p3, conv_w[layer], conv_b[layer][None, :])
        o_fox = _fox(p3, cs, cst, fox_out_norm[layer].reshape(1, FOX_W))
        o_gdn = _gdn(cv, p3, val, cs, cst, gdn_out_norm[layer][None, :])
        ssm_par = _row_tile([jnp.repeat(ssm_D[layer], SSM_HEAD_DIM), ssm_out_norm[layer]], SSM_W)
        o_ssm = _ssd(cv, p3, val, cs, cst, ssm_par)
        wo = w_out_b[layer]
        xf = _outproj(o_fox.reshape(n, FOX_W), o_gdn.reshape(n, GDN_W), o_ssm.reshape(n, SSM_W),
                      wo[:FOX_W], wo[FOX_W:FOX_W + GDN_W], wo[FOX_W + GDN_W:], xf)
        j = layer // 2
        gain = norm_ffn[layer][None, :]
        if layer % 2 == 0:
            xf = _ffn(xf, gain, ffn_g[j], ffn_u[j], ffn_d[j])
        else:
            xf = _moe(xf, gain, moe_router[j], moe_w_gate[j], moe_w_up[j], moe_w_down[j])
    return _final_norm(xf, norm_final[None, :]).reshape(b, s, d)
```

```python
import functools

import jax
import jax.numpy as jnp
from jax import lax
from jax.experimental import pallas as pl
from jax.experimental.pallas import tpu as pltpu

F32 = jnp.float32
BF16 = jnp.bfloat16
I32 = jnp.int32

NORM_EPS = 1e-6
LANES = 128
NEG_BIG = -1e30

FOX_HEADS = 4
HEAD_DIM = 128
GDN_HEADS = 6
GDN_CHUNK = 64
SSM_HEADS = 12
SSM_HEAD_DIM = 64
SSM_STATE = 128
SSM_GROUPS = 2
SSM_CHUNK = 128
CONV_WIDTH = 4
N_EXPERTS = 8
TOP_K = 2

FOX_W = FOX_HEADS * HEAD_DIM
GDN_W = GDN_HEADS * HEAD_DIM
SSM_W = SSM_HEADS * SSM_HEAD_DIM
BC_W = SSM_GROUPS * SSM_STATE
PROJ_SIZES = (FOX_W, FOX_W, FOX_W, FOX_HEADS, 3 * GDN_W, GDN_W, GDN_HEADS, GDN_HEADS,
              SSM_W, SSM_W + 2 * BC_W, SSM_HEADS)

P_FQ, P_FK, P_FV = 0, FOX_W, 2 * FOX_W
P_GQKV = 3 * FOX_W
P_GZ = P_GQKV + 3 * GDN_W
P_SZ = P_GZ + GDN_W
P_SXBC = P_SZ + SSM_W
P_SMALL = P_SXBC + SSM_W + 2 * BC_W
P_WIDTH = 6912
S_FF, S_GB, S_GA, S_DT = 0, 4, 10, 16
S_END = S_DT + SSM_HEADS
C_GQ, C_GK, C_GV = 0, GDN_W, 2 * GDN_W
C_SX = 3 * GDN_W
C_SB = C_SX + SSM_W
C_SC = C_SB + BC_W
C_WIDTH = C_SC + BC_W

MOE_ROWS = 1024
GATHER_ROWS = 256
VMEM_LIMIT = 56 * 1024 * 1024


def _tile(n, pref):
    t = min(n, pref)
    while n % t:
        t //= 2
    return t


def _params(sem, vmem=VMEM_LIMIT):
    return pltpu.CompilerParams(dimension_semantics=sem, vmem_limit_bytes=vmem)


def _silu(x):
    return x / (1.0 + jnp.exp(-x))


def _softplus(x):
    return jnp.maximum(x, 0.0) + jnp.log1p(jnp.exp(-jnp.abs(x)))


def _rms(x, gain):
    return x * lax.rsqrt(jnp.mean(x * x, axis=-1, keepdims=True) + NORM_EPS) * gain


def _dot(a, b):
    return jnp.dot(a, b, preferred_element_type=F32)


def _dot_nt(a, b):
    return lax.dot_general(a, b, (((1,), (1,)), ((), ())), preferred_element_type=F32)


def _dot_tn(a, b):
    return lax.dot_general(a, b, (((0,), (0,)), ((), ())), preferred_element_type=F32)


def _norm_matmul_kernel(x_ref, g_ref, w_ref, o_ref, h_ref):
    @pl.when(pl.program_id(1) == 0)
    def _():
        h_ref[...] = _rms(x_ref[...], g_ref[...]).astype(BF16)

    o_ref[...] = _dot(h_ref[...], w_ref[0])


def _norm_matmul(x, gain, w, layer):
    n, d = x.shape
    nout = w.shape[2]
    tm = _tile(n, 1024)
    tn = _tile(nout, 768)
    return pl.pallas_call(
        _norm_matmul_kernel,
        out_shape=jax.ShapeDtypeStruct((n, nout), F32),
        grid=(n // tm, nout // tn),
        in_specs=[pl.BlockSpec((tm, d), lambda i, j: (i, 0)),
                  pl.BlockSpec((1, d), lambda i, j: (0, 0)),
                  pl.BlockSpec((1, d, tn), lambda i, j: (layer, 0, j))],
        out_specs=pl.BlockSpec((tm, tn), lambda i, j: (i, j)),
        scratch_shapes=[pltpu.VMEM((tm, d), BF16)],
        compiler_params=_params(("parallel", "arbitrary")),
        name="norm_inproj",
    )(x, gain, w)


def _prep_kernel(p_ref, par_ref, val_ref, cs_ref, cst_ref, carry_ref):
    @pl.when(pl.program_id(1) == 0)
    def _():
        carry_ref[...] = jnp.zeros_like(carry_ref)

    blk = p_ref.shape[1]
    v = p_ref[0] + par_ref[0:1, :]
    lane = lax.broadcasted_iota(I32, v.shape, 1)
    sp = _softplus(v)
    log_f = -_softplus(-v)
    beta = 1.0 / (1.0 + jnp.exp(-v))
    neg_a = -jnp.exp(par_ref[1:2, :])
    val_ref[0] = jnp.where(lane < S_GA, beta, sp)
    z = jnp.where(lane < S_GB, log_f, jnp.where(lane < S_GA, 0.0, neg_a * sp))
    z = jnp.where(lane < S_END, z, 0.0)
    row = lax.broadcasted_iota(I32, (blk, blk), 0)
    col = lax.broadcasted_iota(I32, (blk, blk), 1)
    tri = row >= col
    tri_blk = jnp.where(tri, 1.0, 0.0)
    tri_gdn = jnp.where(tri & (row // GDN_CHUNK == col // GDN_CHUNK), 1.0, 0.0)
    cs_blk = jnp.dot(tri_blk, z, precision=lax.Precision.HIGHEST, preferred_element_type=F32)
    cs_gdn = jnp.dot(tri_gdn, z, precision=lax.Precision.HIGHEST, preferred_element_type=F32)
    cs_run = cs_blk + carry_ref[...]
    carry_ref[...] = cs_run[blk - 1:blk, :]
    cs = jnp.where(lane < S_GB, cs_run, jnp.where(lane < S_DT, cs_gdn, cs_blk))
    cs_ref[0] = cs
    cst_ref[0] = cs.T


def _prep(p3, par):
    b, s, _ = p3.shape
    blk = SSM_CHUNK
    shp = jax.ShapeDtypeStruct((b, s, LANES), F32)
    return pl.pallas_call(
        _prep_kernel,
        out_shape=(shp, shp, jax.ShapeDtypeStruct((b, LANES, s), F32)),
        grid=(b, s // blk),
        in_specs=[pl.BlockSpec((1, blk, LANES), lambda bi, i: (bi, i, P_SMALL // LANES)),
                  pl.BlockSpec((8, LANES), lambda bi, i: (0, 0))],
        out_specs=(pl.BlockSpec((1, blk, LANES), lambda bi, i: (bi, i, 0)),
                   pl.BlockSpec((1, blk, LANES), lambda bi, i: (bi, i, 0)),
                   pl.BlockSpec((1, LANES, blk), lambda bi, i: (bi, 0, i))),
        scratch_shapes=[pltpu.VMEM((1, LANES), F32)],
        compiler_params=_params(("parallel", "arbitrary")),
        name="gate_prep",
    )(p3, par)


CONV_COLS = 256
CONV_GDN_BLOCKS = 3 * GDN_W // CONV_COLS
CONV_L2_BLOCKS = 2 * GDN_W // CONV_COLS
CONV_Q_BLOCKS = GDN_W // CONV_COLS


def _conv_kernel(u_ref, halo_ref, w_ref, b_ref, o_ref):
    i = pl.program_id(1)
    c = pl.program_id(2)
    u = u_ref[0]
    t = u.shape[0]
    halo = jnp.where(i > 0, halo_ref[0], 0.0)
    row8 = lax.broadcasted_iota(I32, halo.shape, 0)
    acc = u * w_ref[CONV_WIDTH - 1:CONV_WIDTH, :]
    head = u[0:8] * w_ref[CONV_WIDTH - 1:CONV_WIDTH, :]
    for s in range(1, CONV_WIDTH):
        wk = w_ref[CONV_WIDTH - 1 - s:CONV_WIDTH - s, :]
        rolled = pltpu.roll(u, s, 0)
        acc = acc + rolled * wk
        fixed = jnp.where(row8 < s, pltpu.roll(halo, s, 0), rolled[0:8])
        head = head + fixed * wk
    y = jnp.concatenate([head, acc[8:]], axis=0) if t > 8 else head
    y = _silu(y + b_ref[...])
    parts = []
    for k in range(CONV_COLS // HEAD_DIM):
        yk = y[:, k * HEAD_DIM:(k + 1) * HEAD_DIM]
        nk = yk * lax.rsqrt(jnp.sum(yk * yk, axis=-1, keepdims=True) + NORM_EPS)
        nk = nk * jnp.where(c < CONV_Q_BLOCKS, HEAD_DIM ** -0.5, 1.0)
        parts.append(jnp.where(c < CONV_L2_BLOCKS, nk, yk))
    o_ref[0] = jnp.concatenate(parts, axis=1)


def _conv(p3, cw, cb):
    b, s, _ = p3.shape
    t = _tile(s, 1024)
    gdn0 = P_GQKV // CONV_COLS
    ssm_shift = P_SXBC // CONV_COLS - CONV_GDN_BLOCKS

    def col(c):
        return jnp.where(c < CONV_GDN_BLOCKS, c + gdn0, c + ssm_shift)

    return pl.pallas_call(
        _conv_kernel,
        out_shape=jax.ShapeDtypeStruct((b, s, C_WIDTH), F32),
        grid=(b, s // t, C_WIDTH // CONV_COLS),
        in_specs=[pl.BlockSpec((1, t, CONV_COLS), lambda bi, i, c: (bi, i, col(c))),
                  pl.BlockSpec((1, 8, CONV_COLS),
                               lambda bi, i, c: (bi, jnp.maximum(i * (t // 8) - 1, 0), col(c))),
                  pl.BlockSpec((CONV_WIDTH, CONV_COLS), lambda bi, i, c: (0, c)),
                  pl.BlockSpec((1, CONV_COLS), lambda bi, i, c: (0, c))],
        out_specs=pl.BlockSpec((1, t, CONV_COLS), lambda bi, i, c: (bi, i, c)),
        compiler_params=_params(("parallel", "parallel", "parallel")),
        name="conv_silu",
    )(p3, p3, cw, cb)


def _fox_kernel(q_ref, k_ref, v_ref, ck_ref, gn_ref, o_ref, m_ref, acc_ref):
    i = pl.program_id(1)
    j = pl.program_id(2)
    tq = q_ref.shape[1]
    tk = k_ref.shape[1]
    cols = lambda h: slice(h * HEAD_DIM, (h + 1) * HEAD_DIM)
    wide = lambda h: slice(2 * h * HEAD_DIM, 2 * (h + 1) * HEAD_DIM)

    @pl.when(j == 0)
    def _():
        m_ref[...] = jnp.full_like(m_ref, NEG_BIG)
        acc_ref[...] = jnp.zeros_like(acc_ref)

    def step(masked):
        if masked:
            causal = (lax.broadcasted_iota(I32, (tq, tk), 1) <= lax.broadcasted_iota(I32, (tq, tk), 0))
        ones = jnp.ones((tk, HEAD_DIM), BF16)
        s, p, alpha = {}, {}, {}

        def logits(h):
            q = (q_ref[0, :, cols(h)] * HEAD_DIM ** -0.5).astype(BF16)
            sh = _dot_nt(q, k_ref[0, :, cols(h)].astype(BF16)) - ck_ref[0, h:h + 1, :]
            s[h] = jnp.where(causal, sh, NEG_BIG) if masked else sh

        def softmax(h):
            m_prev = m_ref[h]
            m_new = jnp.maximum(m_prev, jnp.max(s[h], axis=-1, keepdims=True))
            p[h] = jnp.exp(s[h] - jnp.tile(m_new, (1, tk // LANES))).astype(BF16)
            alpha[h] = jnp.exp(m_prev - m_new)
            m_ref[h] = m_new

        def values(h):
            v1 = jnp.concatenate([v_ref[0, :, cols(h)].astype(BF16), ones], axis=1)
            acc_ref[:, wide(h)] = jnp.tile(alpha[h], (1, 2)) * acc_ref[:, wide(h)] + _dot(p[h], v1)

        for t in range(FOX_HEADS + 2):
            if t < FOX_HEADS:
                logits(t)
            if 0 <= t - 1 < FOX_HEADS:
                softmax(t - 1)
            if 0 <= t - 2 < FOX_HEADS:
                values(t - 2)

    @pl.when(j < i)
    def _():
        step(False)

    @pl.when(j == i)
    def _():
        step(True)

    @pl.when(j == pl.num_programs(2) - 1)
    def _():
        for h in range(FOX_HEADS):
            both = acc_ref[:, wide(h)]
            o = both[:, :HEAD_DIM] / both[:, HEAD_DIM:]
            o_ref[0, :, cols(h)] = _rms(o, gn_ref[:, cols(h)]).astype(o_ref.dtype)


def _fox(p3, cst, gain):
    b, s, _ = p3.shape
    t = _tile(s, 512)
    n = s // t
    return pl.pallas_call(
        _fox_kernel,
        out_shape=jax.ShapeDtypeStruct((b, s, FOX_W), BF16),
        grid=(b, n, n),
        in_specs=[pl.BlockSpec((1, t, FOX_W), lambda bi, i, j: (bi, i, P_FQ // FOX_W)),
                  pl.BlockSpec((1, t, FOX_W), lambda bi, i, j: (bi, jnp.minimum(i, j), P_FK // FOX_W)),
                  pl.BlockSpec((1, t, FOX_W), lambda bi, i, j: (bi, jnp.minimum(i, j), P_FV // FOX_W)),
                  pl.BlockSpec((1, 8, t), lambda bi, i, j: (bi, 0, jnp.minimum(i, j))),
                  pl.BlockSpec((1, FOX_W), lambda bi, i, j: (0, 0))],
        out_specs=pl.BlockSpec((1, t, FOX_W), lambda bi, i, j: (bi, i, 0)),
        scratch_shapes=[pltpu.VMEM((FOX_HEADS, t, LANES), F32),
                        pltpu.VMEM((t, 2 * FOX_W), F32)],
        compiler_params=_params(("parallel", "parallel", "arbitrary")),
        name="fox_attention",
    )(p3, p3, p3, cst, gain)


def _gdn_kernel(q_ref, k_ref, v_ref, z_ref, val_ref, cs_ref, cst_ref, gn_ref, o_ref, state_ref):
    @pl.when(pl.program_id(1) == 0)
    def _():
        state_ref[...] = jnp.zeros_like(state_ref)

    c = GDN_CHUNK
    n_chunks = q_ref.shape[1] // c
    row = lax.broadcasted_iota(I32, (c, c), 0)
    col = lax.broadcasted_iota(I32, (c, c), 1)
    incl = row >= col
    strict = row > col
    eye = jnp.where(row == col, 1.0, 0.0)
    pairs = [(ci, h) for ci in range(n_chunks) for h in range(GDN_HEADS)]

    def rows(ci):
        return slice(ci * c, (ci + 1) * c)

    def cols(h):
        return slice(h * HEAD_DIM, (h + 1) * HEAD_DIM)

    q, k, kb, decay, eg, gc, rhs = {}, {}, {}, {}, {}, {}, {}
    for ci, h in pairs:
        g = (ci, h)
        q[g] = q_ref[0, rows(ci), cols(h)]
        k[g] = k_ref[0, rows(ci), cols(h)]
        beta = val_ref[0, rows(ci), S_GB + h:S_GB + h + 1]
        gc[g] = cs_ref[0, rows(ci), S_GA + h:S_GA + h + 1]
        gr = cst_ref[0, S_GA + h:S_GA + h + 1, rows(ci)]
        decay[g] = jnp.where(incl, jnp.exp(jnp.where(incl, gc[g] - gr, 0.0)), 0.0)
        eg[g] = jnp.exp(gc[g])
        kb[g] = k[g] * beta
        rhs[g] = jnp.concatenate([v_ref[0, rows(ci), cols(h)] * beta, kb[g] * eg[g]], axis=1)
    a = {g: jnp.where(strict, _dot_nt(kb[g], k[g]) * decay[g], 0.0) for g in pairs}
    qk = {g: jnp.where(incl, _dot_nt(q[g], k[g]) * decay[g], 0.0) for g in pairs}
    off = (row // 2 == col // 2) & (row != col)
    tinv = {g: eye - jnp.where(off, a[g], 0.0) for g in pairs}
    sz = 2
    while sz < c:
        off = (row // (2 * sz) == col // (2 * sz)) & (row // sz != col // sz)
        left = {g: _dot(tinv[g], jnp.where(off, a[g], 0.0)) for g in pairs}
        tinv = {g: tinv[g] - _dot(left[g], tinv[g]) for g in pairs}
        sz *= 2
    sol = {g: _dot(tinv[g], rhs[g]) for g in pairs}
    state = [state_ref[h] for h in range(GDN_HEADS)]
    heads = range(GDN_HEADS)
    for ci in range(n_chunks):
        ws = [_dot(sol[ci, h][:, HEAD_DIM:], state[h]) for h in heads]
        qs = [_dot(q[ci, h] * eg[ci, h], state[h]) for h in heads]
        v_new = [sol[ci, h][:, :HEAD_DIM] - ws[h] for h in heads]
        intra = [_dot(qk[ci, h], v_new[h]) for h in heads]
        for h in heads:
            g_last = gc[ci, h][c - 1:c, :]
            k_dec = k[ci, h] * jnp.exp(g_last - gc[ci, h])
            state[h] = state[h] * jnp.exp(g_last) + _dot_tn(k_dec, v_new[h])
        for h in heads:
            o = _rms(qs[h] + intra[h], gn_ref[...]) * _silu(z_ref[0, rows(ci), cols(h)])
            o_ref[0, rows(ci), cols(h)] = o.astype(o_ref.dtype)
    for h in heads:
        state_ref[h] = state[h]


def _gdn(cv, p3, val, cs, cst, gain):
    b, s, _ = cv.shape
    t = SSM_CHUNK
    wide = lambda blk: pl.BlockSpec((1, t, GDN_W), lambda bi, i: (bi, i, blk))
    small = pl.BlockSpec((1, t, LANES), lambda bi, i: (bi, i, 0))
    return pl.pallas_call(
        _gdn_kernel,
        out_shape=jax.ShapeDtypeStruct((b, s, GDN_W), BF16),
        grid=(b, s // t),
        in_specs=[wide(C_GQ // GDN_W), wide(C_GK // GDN_W), wide(C_GV // GDN_W),
                  wide(P_GZ // GDN_W), small, small,
                  pl.BlockSpec((1, LANES, t), lambda bi, i: (bi, 0, i)),
                  pl.BlockSpec((1, HEAD_DIM), lambda bi, i: (0, 0))],
        out_specs=pl.BlockSpec((1, t, GDN_W), lambda bi, i: (bi, i, 0)),
        scratch_shapes=[pltpu.VMEM((GDN_HEADS, HEAD_DIM, HEAD_DIM), F32)],
        compiler_params=_params(("parallel", "arbitrary")),
        name="gated_deltanet",
    )(cv, cv, cv, p3, val, cs, cst, gain)


SSM_PAIRS = SSM_HEADS // 2
PAIRS_PER_GROUP = SSM_PAIRS // SSM_GROUPS


def _ssd_kernel(x_ref, b_ref, c_ref, z_ref, val_ref, cs_ref, cst_ref, par_ref, o_ref, st_ref):
    @pl.when(pl.program_id(1) == 0)
    def _():
        st_ref[...] = jnp.zeros_like(st_ref)

    n = x_ref.shape[1]
    row = lax.broadcasted_iota(I32, (n, n), 0)
    col = lax.broadcasted_iota(I32, (n, n), 1)
    incl = row >= col
    lo = lax.broadcasted_iota(I32, (n, LANES), 1) < SSM_HEAD_DIM
    lo_row = lo[0:1, :]

    def decay_of(h):
        ac = cs_ref[0, :, S_DT + h:S_DT + h + 1]
        ar = cst_ref[0, S_DT + h:S_DT + h + 1, :]
        return jnp.where(incl, jnp.exp(jnp.where(incl, ac - ar, 0.0)), 0.0), ac

    groups = range(SSM_GROUPS)
    pairs = range(SSM_PAIRS)
    lanes_of = lambda p: slice(p * LANES, (p + 1) * LANES)
    bg = [b_ref[0, :, g * SSM_STATE:(g + 1) * SSM_STATE] for g in groups]
    cg = [c_ref[0, :, g * SSM_STATE:(g + 1) * SSM_STATE] for g in groups]
    cb = [_dot_nt(cg[g], bg[g]) for g in groups]
    bg_t = [bg[g].T for g in groups]
    xp, xdt, dec, e_ac, tail, st_dec = [], [], [], [], [], []
    for p in pairs:
        h0, h1 = 2 * p, 2 * p + 1
        xp.append(x_ref[0, :, lanes_of(p)])
        dt = jnp.where(lo, val_ref[0, :, S_DT + h0:S_DT + h0 + 1],
                       val_ref[0, :, S_DT + h1:S_DT + h1 + 1])
        xdt.append(xp[p] * dt)
        d0, ac0 = decay_of(h0)
        d1, ac1 = decay_of(h1)
        dec.append((d0, d1))
        e_ac.append(jnp.where(lo, jnp.exp(ac0), jnp.exp(ac1)))
        al0 = ac0[n - 1:n, :]
        al1 = ac1[n - 1:n, :]
        tail.append(jnp.where(lo, jnp.exp(al0 - ac0), jnp.exp(al1 - ac1)) * xdt[p])
        st_dec.append(jnp.where(lo_row, jnp.exp(al0), jnp.exp(al1)))
    st = [st_ref[p] for p in pairs]
    grp = lambda p: p // PAIRS_PER_GROUP
    diag0 = [_dot(cb[grp(p)] * dec[p][0], xdt[p]) for p in pairs]
    diag1 = [_dot(cb[grp(p)] * dec[p][1], xdt[p]) for p in pairs]
    y_off = [_dot(cg[grp(p)], st[p]) for p in pairs]
    st_add = [_dot(bg_t[grp(p)], tail[p]) for p in pairs]
    ys = []
    for p in pairs:
        st_ref[p] = st[p] * st_dec[p] + st_add[p]
        y_diag = jnp.where(lo, diag0[p], diag1[p])
        ys.append((y_diag + y_off[p] * e_ac[p] + par_ref[0:1, lanes_of(p)] * xp[p])
                  * _silu(z_ref[0, :, lanes_of(p)]))
    for g in groups:
        mine = [p for p in pairs if grp(p) == g]
        ssq = sum(jnp.sum(ys[p] * ys[p], axis=-1, keepdims=True) for p in mine)
        inv = lax.rsqrt(ssq / (PAIRS_PER_GROUP * LANES) + NORM_EPS)
        for p in mine:
            o_ref[0, :, lanes_of(p)] = (ys[p] * inv * par_ref[1:2, lanes_of(p)]).astype(o_ref.dtype)


def _ssd(cv, p3, val, cs, cst, par):
    b, s, _ = cv.shape
    t = SSM_CHUNK
    small = pl.BlockSpec((1, t, LANES), lambda bi, i: (bi, i, 0))
    return pl.pallas_call(
        _ssd_kernel,
        out_shape=jax.ShapeDtypeStruct((b, s, SSM_W), BF16),
        grid=(b, s // t),
        in_specs=[pl.BlockSpec((1, t, SSM_W), lambda bi, i: (bi, i, C_SX // SSM_W)),
                  pl.BlockSpec((1, t, BC_W), lambda bi, i: (bi, i, C_SB // BC_W)),
                  pl.BlockSpec((1, t, BC_W), lambda bi, i: (bi, i, C_SC // BC_W)),
                  pl.BlockSpec((1, t, SSM_W), lambda bi, i: (bi, i, P_SZ // SSM_W)),
                  small, small,
                  pl.BlockSpec((1, LANES, t), lambda bi, i: (bi, 0, i)),
                  pl.BlockSpec((8, SSM_W), lambda bi, i: (0, 0))],
        out_specs=pl.BlockSpec((1, t, SSM_W), lambda bi, i: (bi, i, 0)),
        scratch_shapes=[pltpu.VMEM((SSM_PAIRS, SSM_STATE, LANES), F32)],
        compiler_params=_params(("parallel", "arbitrary")),
        name="ssd_scan",
    )(cv, cv, cv, p3, val, cs, cst, par)


def _outproj_kernel(a1_ref, a2_ref, a3_ref, w_ref, x_ref, o_ref):
    r1 = a1_ref.shape[1]
    r2 = r1 + a2_ref.shape[1]
    o_ref[...] = (x_ref[...] + _dot(a1_ref[...], w_ref[0, :r1, :]) + _dot(a2_ref[...], w_ref[0, r1:r2, :])
                  + _dot(a3_ref[...], w_ref[0, r2:, :]))


def _outproj(a1, a2, a3, w, layer, x):
    n, d = x.shape
    tm = _tile(n, 1024)
    tn = _tile(d, 512)
    act = lambda a: pl.BlockSpec((tm, a.shape[1]), lambda i, j: (i, 0))
    return pl.pallas_call(
        _outproj_kernel,
        out_shape=jax.ShapeDtypeStruct((n, d), F32),
        grid=(n // tm, d // tn),
        in_specs=[act(a1), act(a2), act(a3),
                  pl.BlockSpec((1, w.shape[1], tn), lambda i, j: (layer, 0, j)),
                  pl.BlockSpec((tm, tn), lambda i, j: (i, j))],
        out_specs=pl.BlockSpec((tm, tn), lambda i, j: (i, j)),
        compiler_params=_params(("parallel", "parallel")),
        name="out_proj",
    )(a1, a2, a3, w, x)


def _ffn_kernel(x_ref, g_ref, wg_ref, wu_ref, wd_ref, o_ref, h_ref, acc_ref):
    f = pl.program_id(1)

    @pl.when(f == 0)
    def _():
        h_ref[...] = _rms(x_ref[...], g_ref[...]).astype(BF16)
        acc_ref[...] = jnp.zeros_like(acc_ref)

    h = h_ref[...]
    act = _silu(_dot(h, wg_ref[0])) * _dot(h, wu_ref[0])
    acc_ref[...] += _dot(act.astype(BF16), wd_ref[0])

    @pl.when(f == pl.num_programs(1) - 1)
    def _():
        o_ref[...] = x_ref[...] + acc_ref[...]


def _ffn(x, gain, wg, wu, wd, layer):
    n, d = x.shape
    f = wg.shape[2]
    tm = _tile(n, 512)
    tf = _tile(f, 512)
    return pl.pallas_call(
        _ffn_kernel,
        out_shape=jax.ShapeDtypeStruct((n, d), F32),
        grid=(n // tm, f // tf),
        in_specs=[pl.BlockSpec((tm, d), lambda i, j: (i, 0)),
                  pl.BlockSpec((1, d), lambda i, j: (0, 0)),
                  pl.BlockSpec((1, d, tf), lambda i, j: (layer, 0, j)),
                  pl.BlockSpec((1, d, tf), lambda i, j: (layer, 0, j)),
                  pl.BlockSpec((1, tf, d), lambda i, j: (layer, j, 0))],
        out_specs=pl.BlockSpec((tm, d), lambda i, j: (i, 0)),
        scratch_shapes=[pltpu.VMEM((tm, d), BF16), pltpu.VMEM((tm, d), F32)],
        compiler_params=_params(("parallel", "arbitrary")),
        name="dense_ffn",
    )(x, gain, wg, wu, wd)


def _router_kernel(x_ref, g_ref, r_ref, idx_ref, w_ref):
    h = _rms(x_ref[...], g_ref[...])
    logits = jnp.dot(h, r_ref[...], precision=lax.Precision.HIGHEST, preferred_element_type=F32)
    lane = lax.broadcasted_iota(I32, logits.shape, 1)
    l1 = jnp.where(lane < N_EXPERTS, logits, NEG_BIG)
    m1 = jnp.max(l1, axis=-1, keepdims=True)
    i1 = jnp.min(jnp.where(l1 == m1, lane, LANES), axis=-1, keepdims=True)
    l2 = jnp.where(lane == i1, NEG_BIG, l1)
    m2 = jnp.max(l2, axis=-1, keepdims=True)
    i2 = jnp.min(jnp.where(l2 == m2, lane, LANES), axis=-1, keepdims=True)
    e = jnp.exp(m2 - m1)
    idx_ref[...] = jnp.where(lane == 0, i1, jnp.where(lane == 1, i2, 0))
    w_ref[...] = jnp.where(lane == 0, 1.0 / (1.0 + e), jnp.where(lane == 1, e / (1.0 + e), 0.0))


def _router(x, gain, router_pad):
    n, d = x.shape
    tm = _tile(n, 512)
    return pl.pallas_call(
        _router_kernel,
        out_shape=(jax.ShapeDtypeStruct((n, LANES), I32), jax.ShapeDtypeStruct((n, LANES), F32)),
        grid=(n // tm,),
        in_specs=[pl.BlockSpec((tm, d), lambda i: (i, 0)),
                  pl.BlockSpec((1, d), lambda i: (0, 0)),
                  pl.BlockSpec((d, LANES), lambda i: (0, 0))],
        out_specs=(pl.BlockSpec((tm, LANES), lambda i: (i, 0)),
                   pl.BlockSpec((tm, LANES), lambda i: (i, 0))),
        compiler_params=_params(("parallel",)),
        name="moe_router",
    )(x, gain, router_pad)


def _row_copy(src_hbm, src_row, dst, dst_row, sem):
    return pltpu.make_async_copy(src_hbm.at[pl.ds(src_row, 1)], dst.at[pl.ds(dst_row, 1)], sem)


DMA_UNROLL = 8


def _gather_norm_kernel(nused_ref, tok_ref, x_hbm, g_ref, o_ref, buf, sem):
    rows = o_ref.shape[0]
    used = pl.program_id(0) < nused_ref[0]

    @pl.when(used)
    def _():
        def start(r, carry):
            _row_copy(x_hbm, tok_ref[0, 0, r], buf, r, sem).start()
            return carry

        def wait(r, carry):
            _row_copy(x_hbm, 0, buf, r, sem).wait()
            return carry

        lax.fori_loop(0, rows, start, 0, unroll=DMA_UNROLL)
        lax.fori_loop(0, rows, wait, 0, unroll=DMA_UNROLL)
        o_ref[...] = _rms(buf[...], g_ref[...]).astype(o_ref.dtype)

    @pl.when(jnp.logical_not(used))
    def _():
        o_ref[...] = jnp.zeros_like(o_ref)


def _gather_norm(x, gain, slot_tok, n_used_blocks):
    n, d = x.shape
    n_slots = slot_tok.shape[0]
    rb = GATHER_ROWS
    grid_spec = pltpu.PrefetchScalarGridSpec(
        num_scalar_prefetch=1,
        grid=(n_slots // rb,),
        in_specs=[pl.BlockSpec((1, 1, rb), lambda i, nu: (i, 0, 0), memory_space=pltpu.SMEM),
                  pl.BlockSpec(memory_space=pl.ANY),
                  pl.BlockSpec((1, d), lambda i, nu: (0, 0))],
        out_specs=pl.BlockSpec((rb, d), lambda i, nu: (i, 0)),
        scratch_shapes=[pltpu.VMEM((rb, d), F32), pltpu.SemaphoreType.DMA(())],
    )
    return pl.pallas_call(
        _gather_norm_kernel,
        out_shape=jax.ShapeDtypeStruct((n_slots, d), BF16),
        grid_spec=grid_spec,
        compiler_params=_params(("arbitrary",)),
        name="moe_gather",
    )(n_used_blocks, slot_tok.reshape(n_slots // rb, 1, rb), x, gain)


def _expert_ffn_kernel(e_ref, na_ref, x_ref, wg_ref, wu_ref, wd_ref, o_ref):
    s = pl.program_id(0)
    f = pl.program_id(1)
    active = s < na_ref[0]

    @pl.when(f == 0)
    def _():
        o_ref[...] = jnp.zeros_like(o_ref)

    @pl.when(active)
    def _():
        x = x_ref[...]
        gate = _dot(x, wg_ref[0, 0].astype(BF16))
        up = _dot(x, wu_ref[0, 0].astype(BF16))
        o_ref[...] += _dot((_silu(gate) * up).astype(BF16), wd_ref[0, 0].astype(BF16))


def _expert_ffn(xs, blk_expert, n_active, wg, wu, wd, layer):
    n_slots, d = xs.shape
    f = wg.shape[3]
    nblk = n_slots // MOE_ROWS
    tf = _tile(f, 256)
    nf = f // tf

    def fidx(s, j, na_ref):
        return jnp.where(s < na_ref[0], j, nf - 1)

    grid_spec = pltpu.PrefetchScalarGridSpec(
        num_scalar_prefetch=2,
        grid=(nblk, nf),
        in_specs=[pl.BlockSpec((MOE_ROWS, d), lambda s, j, e, na: (s, 0)),
                  pl.BlockSpec((1, 1, d, tf), lambda s, j, e, na: (layer, e[s], 0, fidx(s, j, na))),
                  pl.BlockSpec((1, 1, d, tf), lambda s, j, e, na: (layer, e[s], 0, fidx(s, j, na))),
                  pl.BlockSpec((1, 1, tf, d), lambda s, j, e, na: (layer, e[s], fidx(s, j, na), 0))],
        out_specs=pl.BlockSpec((MOE_ROWS, d), lambda s, j, e, na: (s, 0)),
    )
    return pl.pallas_call(
        _expert_ffn_kernel,
        out_shape=jax.ShapeDtypeStruct((n_slots, d), F32),
        grid_spec=grid_spec,
        compiler_params=_params(("arbitrary", "arbitrary")),
        name="moe_expert_ffn",
    )(blk_expert, n_active, xs, wg, wu, wd)


def _combine_kernel(d0_ref, d1_ref, x_ref, w_ref, y_hbm, o_ref, buf, sem):
    rows = o_ref.shape[0]

    def start(r, carry):
        _row_copy(y_hbm, d0_ref[0, 0, r], buf.at[0], r, sem).start()
        _row_copy(y_hbm, d1_ref[0, 0, r], buf.at[1], r, sem).start()
        return carry

    def wait(r, carry):
        _row_copy(y_hbm, 0, buf.at[0], r, sem).wait()
        _row_copy(y_hbm, 0, buf.at[1], r, sem).wait()
        return carry

    lax.fori_loop(0, rows, start, 0, unroll=DMA_UNROLL)
    lax.fori_loop(0, rows, wait, 0, unroll=DMA_UNROLL)
    o_ref[...] = x_ref[...] + w_ref[:, 0:1] * buf[0] + w_ref[:, 1:2] * buf[1]


def _combine(x, top_w, dest0, dest1, ys):
    n, d = x.shape
    tb = _tile(n, 256)
    dspec = pl.BlockSpec((1, 1, tb), lambda i: (i, 0, 0), memory_space=pltpu.SMEM)
    return pl.pallas_call(
        _combine_kernel,
        out_shape=jax.ShapeDtypeStruct((n, d), F32),
        grid=(n // tb,),
        in_specs=[dspec, dspec,
                  pl.BlockSpec((tb, d), lambda i: (i, 0)),
                  pl.BlockSpec((tb, LANES), lambda i: (i, 0)),
                  pl.BlockSpec(memory_space=pl.ANY)],
        out_specs=pl.BlockSpec((tb, d), lambda i: (i, 0)),
        scratch_shapes=[pltpu.VMEM((2, tb, d), F32), pltpu.SemaphoreType.DMA(())],
        compiler_params=_params(("arbitrary",)),
        name="moe_combine",
    )(dest0.reshape(n // tb, 1, tb), dest1.reshape(n // tb, 1, tb), x, top_w, ys)


def _moe(x, gain, router, wg, wu, wd, layer):
    n, d = x.shape
    router_pad = jnp.pad(router, ((0, 0), (0, LANES - N_EXPERTS)))
    top_idx, top_w = _router(x, gain, router_pad)
    flat_e = top_idx[:, :TOP_K].reshape(-1)
    onehot = (flat_e[:, None] == jnp.arange(N_EXPERTS, dtype=I32)[None, :]).astype(I32)
    rank = jnp.take_along_axis(jnp.cumsum(onehot, axis=0) - onehot, flat_e[:, None], axis=1)[:, 0]
    counts = jnp.sum(onehot, axis=0)
    padded = (counts + MOE_ROWS - 1) // MOE_ROWS * MOE_ROWS
    pad_end = jnp.cumsum(padded)
    dest = (pad_end - padded)[flat_e] + rank
    nblk = (n * TOP_K) // MOE_ROWS + N_EXPERTS
    n_slots = nblk * MOE_ROWS
    slot_tok = jnp.zeros((n_slots,), I32).at[dest].set(jnp.arange(n * TOP_K, dtype=I32) // TOP_K)
    n_active = (pad_end[-1] // MOE_ROWS).astype(I32)
    blk_start = jnp.arange(nblk, dtype=I32) * MOE_ROWS
    blk_expert = jnp.minimum(jnp.searchsorted(pad_end, blk_start, side='right'), N_EXPERTS - 1)
    last_e = blk_expert[jnp.maximum(n_active - 1, 0)]
    blk_expert = jnp.where(jnp.arange(nblk) < n_active, blk_expert, last_e).astype(I32)

    xs = _gather_norm(x, gain, slot_tok, (n_active * (MOE_ROWS // GATHER_ROWS)).reshape(1))
    ys = _expert_ffn(xs, blk_expert, n_active.reshape(1), wg, wu, wd, layer)
    dest2 = dest.reshape(n, TOP_K)
    return _combine(x, top_w, dest2[:, 0], dest2[:, 1], ys)


def _final_norm_kernel(x_ref, g_ref, o_ref):
    o_ref[...] = _rms(x_ref[...], g_ref[...])


def _final_norm(x, gain):
    n, d = x.shape
    tm = _tile(n, 512)
    return pl.pallas_call(
        _final_norm_kernel,
        out_shape=jax.ShapeDtypeStruct((n, d), F32),
        grid=(n // tm,),
        in_specs=[pl.BlockSpec((tm, d), lambda i: (i, 0)), pl.BlockSpec((1, d), lambda i: (0, 0))],
        out_specs=pl.BlockSpec((tm, d), lambda i: (i, 0)),
        compiler_params=_params(("parallel",)),
        name="final_norm",
    )(x, gain)


def _layout_w_in(w_in):
    offs = [0]
    for sz in PROJ_SIZES:
        offs.append(offs[-1] + sz)
    seg = [w_in[..., offs[k]:offs[k + 1]] for k in range(len(PROJ_SIZES))]
    fq, fk, fv, ff, gqkv, gz, gb, ga, sz_, sxbc, sdt = seg
    lead = w_in.shape[:-1]
    small = jnp.concatenate([ff, gb, ga, sdt, jnp.zeros(lead + (LANES - S_END,), w_in.dtype)], -1)
    tail = jnp.zeros(lead + (P_WIDTH - P_SMALL - LANES,), w_in.dtype)
    return jnp.concatenate([fq, fk, fv, gqkv, gz, sz_, sxbc, small, tail], -1).astype(BF16)


def _row_tile(rows, width):
    out = jnp.zeros((8, width), F32)
    for r, v in enumerate(rows):
        out = out.at[r, :v.shape[0]].set(v.astype(F32))
    return out


def kernel(x, norm_mix, w_in, fox_f_bias, fox_out_norm, gdn_conv_w, gdn_A_log, gdn_dt_bias,
           gdn_out_norm, ssm_conv_w, ssm_conv_b, ssm_A_log, ssm_dt_bias, ssm_D, ssm_out_norm,
           w_out, norm_ffn, ffn_w_gate, ffn_w_up, ffn_w_down, moe_router, moe_w_gate,
           moe_w_up, moe_w_down, norm_final):
    b, s, d = x.shape
    depth = w_in.shape[0]
    n = b * s
    w_in_l = _layout_w_in(w_in)
    w_out_b = w_out.astype(BF16)
    ffn_g, ffn_u, ffn_d = (w.astype(BF16) for w in (ffn_w_gate, ffn_w_up, ffn_w_down))
    conv_w = jnp.concatenate([gdn_conv_w, ssm_conv_w], axis=-1)
    conv_b = jnp.concatenate([jnp.zeros((depth, 3 * GDN_W), F32), ssm_conv_b], axis=-1)

    xf = x.reshape(n, d)
    for layer in range(depth):
        p = _norm_matmul(xf, norm_mix[layer][None, :], w_in_l, layer)
        p3 = p.reshape(b, s, P_WIDTH)
        zeros = lambda k: jnp.zeros((k,), F32)
        bias_row = jnp.concatenate([fox_f_bias[layer], zeros(GDN_HEADS), gdn_dt_bias[layer],
                                    ssm_dt_bias[layer]])
        alog_row = jnp.concatenate([zeros(S_GA), gdn_A_log[layer], ssm_A_log[layer]])
        val, cs, cst = _prep(p3, _row_tile([bias_row, alog_row], LANES))
        cv = _conv(p3, conv_w[layer], conv_b[layer][None, :])
        o_fox = _fox(p3, cst, fox_out_norm[layer].reshape(1, FOX_W))
        o_gdn = _gdn(cv, p3, val, cs, cst, gdn_out_norm[layer][None, :])
        ssm_par = _row_tile([jnp.repeat(ssm_D[layer], SSM_HEAD_DIM), ssm_out_norm[layer]], SSM_W)
        o_ssm = _ssd(cv, p3, val, cs, cst, ssm_par)
        xf = _outproj(o_fox.reshape(n, FOX_W), o_gdn.reshape(n, GDN_W), o_ssm.reshape(n, SSM_W),
                      w_out_b, layer, xf)
        j = layer // 2
        gain = norm_ffn[layer][None, :]
        if layer % 2 == 0:
            xf = _ffn(xf, gain, ffn_g, ffn_u, ffn_d, j)
        else:
            xf = _moe(xf, gain, moe_router[j], moe_w_gate, moe_w_up, moe_w_down, j)
    return _final_norm(xf, norm_final[None, :]).reshape(b, s, d)
```

```python
import functools

import jax
import jax.numpy as jnp
from jax import lax
from jax.experimental import pallas as pl
from jax.experimental.pallas import tpu as pltpu

F32 = jnp.float32
BF16 = jnp.bfloat16
I32 = jnp.int32

NORM_EPS = 1e-6
LANES = 128
NEG_BIG = -1e30

FOX_HEADS = 4
HEAD_DIM = 128
GDN_HEADS = 6
GDN_CHUNK = 64
GDN_ROWS = 256
SSM_HEADS = 12
SSM_HEAD_DIM = 64
SSM_STATE = 128
SSM_GROUPS = 2
SSM_CHUNK = 128
SSD_ROWS = 256
CONV_WIDTH = 4
N_EXPERTS = 8
TOP_K = 2

FOX_W = FOX_HEADS * HEAD_DIM
GDN_W = GDN_HEADS * HEAD_DIM
SSM_W = SSM_HEADS * SSM_HEAD_DIM
BC_W = SSM_GROUPS * SSM_STATE
PROJ_SIZES = (FOX_W, FOX_W, FOX_W, FOX_HEADS, 3 * GDN_W, GDN_W, GDN_HEADS, GDN_HEADS,
              SSM_W, SSM_W + 2 * BC_W, SSM_HEADS)

P_FQ, P_FK, P_FV = 0, FOX_W, 2 * FOX_W
P_GQKV = 3 * FOX_W
P_GZ = P_GQKV + 3 * GDN_W
P_SZ = P_GZ + GDN_W
P_SXBC = P_SZ + SSM_W
P_SMALL = P_SXBC + SSM_W + 2 * BC_W
P_WIDTH = 6912
S_FF, S_GB, S_GA, S_DT = 0, 4, 10, 16
S_END = S_DT + SSM_HEADS
C_GQ, C_GK, C_GV = 0, GDN_W, 2 * GDN_W
C_SX = 3 * GDN_W
C_SB = C_SX + SSM_W
C_SC = C_SB + BC_W
C_WIDTH = C_SC + BC_W

MOE_ROWS = 1024
MOE_SUB = 256
GATHER_ROWS = 256
VMEM_LIMIT = 56 * 1024 * 1024


def _tile(n, pref):
    t = min(n, pref)
    while n % t:
        t //= 2
    return t


def _params(sem, vmem=VMEM_LIMIT):
    return pltpu.CompilerParams(dimension_semantics=sem, vmem_limit_bytes=vmem)


def _silu(x):
    return x / (1.0 + jnp.exp(-x))


def _softplus(x):
    return jnp.maximum(x, 0.0) + jnp.log1p(jnp.exp(-jnp.abs(x)))


def _rms(x, gain):
    return x * lax.rsqrt(jnp.mean(x * x, axis=-1, keepdims=True) + NORM_EPS) * gain


def _dot(a, b):
    return jnp.dot(a, b, preferred_element_type=F32)


def _dot_nt(a, b):
    return lax.dot_general(a, b, (((1,), (1,)), ((), ())), preferred_element_type=F32)


def _dot_tn(a, b):
    return lax.dot_general(a, b, (((0,), (0,)), ((), ())), preferred_element_type=F32)


def _norm_matmul_kernel(x_ref, g_ref, w_ref, o_ref, h_ref):
    @pl.when(pl.program_id(1) == 0)
    def _():
        h_ref[...] = _rms(x_ref[...], g_ref[...]).astype(BF16)

    o_ref[...] = _dot(h_ref[...], w_ref[0])


def _norm_matmul(x, gain, w, layer):
    n, d = x.shape
    nout = w.shape[2]
    tm = _tile(n, 1024)
    tn = _tile(nout, 768)
    return pl.pallas_call(
        _norm_matmul_kernel,
        out_shape=jax.ShapeDtypeStruct((n, nout), F32),
        grid=(n // tm, nout // tn),
        in_specs=[pl.BlockSpec((tm, d), lambda i, j: (i, 0)),
                  pl.BlockSpec((1, d), lambda i, j: (0, 0)),
                  pl.BlockSpec((1, d, tn), lambda i, j: (layer, 0, j))],
        out_specs=pl.BlockSpec((tm, tn), lambda i, j: (i, j)),
        scratch_shapes=[pltpu.VMEM((tm, d), BF16)],
        compiler_params=_params(("parallel", "arbitrary")),
        name="norm_inproj",
    )(x, gain, w)


def _prep_kernel(p_ref, par_ref, val_ref, cs_ref, cst_ref, carry_ref):
    @pl.when(pl.program_id(1) == 0)
    def _():
        carry_ref[...] = jnp.zeros_like(carry_ref)

    blk = p_ref.shape[1]
    v = p_ref[0] + par_ref[0:1, :]
    lane = lax.broadcasted_iota(I32, v.shape, 1)
    sp = _softplus(v)
    log_f = -_softplus(-v)
    beta = 1.0 / (1.0 + jnp.exp(-v))
    neg_a = -jnp.exp(par_ref[1:2, :])
    val_ref[0] = jnp.where(lane < S_GA, beta, sp)
    z = jnp.where(lane < S_GB, log_f, jnp.where(lane < S_GA, 0.0, neg_a * sp))
    z = jnp.where(lane < S_END, z, 0.0)
    row = lax.broadcasted_iota(I32, (blk, blk), 0)
    col = lax.broadcasted_iota(I32, (blk, blk), 1)
    tri = row >= col
    tri_blk = jnp.where(tri, 1.0, 0.0)
    tri_gdn = jnp.where(tri & (row // GDN_CHUNK == col // GDN_CHUNK), 1.0, 0.0)
    cs_blk = jnp.dot(tri_blk, z, precision=lax.Precision.HIGHEST, preferred_element_type=F32)
    cs_gdn = jnp.dot(tri_gdn, z, precision=lax.Precision.HIGHEST, preferred_element_type=F32)
    cs_run = cs_blk + carry_ref[...]
    carry_ref[...] = cs_run[blk - 1:blk, :]
    cs = jnp.where(lane < S_GB, cs_run, jnp.where(lane < S_DT, cs_gdn, cs_blk))
    cs_ref[0] = cs
    cst_ref[0] = cs.T


def _prep(p3, par):
    b, s, _ = p3.shape
    blk = SSM_CHUNK
    shp = jax.ShapeDtypeStruct((b, s, LANES), F32)
    return pl.pallas_call(
        _prep_kernel,
        out_shape=(shp, shp, jax.ShapeDtypeStruct((b, LANES, s), F32)),
        grid=(b, s // blk),
        in_specs=[pl.BlockSpec((1, blk, LANES), lambda bi, i: (bi, i, P_SMALL // LANES)),
                  pl.BlockSpec((8, LANES), lambda bi, i: (0, 0))],
        out_specs=(pl.BlockSpec((1, blk, LANES), lambda bi, i: (bi, i, 0)),
                   pl.BlockSpec((1, blk, LANES), lambda bi, i: (bi, i, 0)),
                   pl.BlockSpec((1, LANES, blk), lambda bi, i: (bi, 0, i))),
        scratch_shapes=[pltpu.VMEM((1, LANES), F32)],
        compiler_params=_params(("parallel", "arbitrary")),
        name="gate_prep",
    )(p3, par)


CONV_COLS = 256
CONV_GDN_BLOCKS = 3 * GDN_W // CONV_COLS
CONV_L2_BLOCKS = 2 * GDN_W // CONV_COLS
CONV_Q_BLOCKS = GDN_W // CONV_COLS


def _conv_kernel(u_ref, halo_ref, w_ref, b_ref, o_ref):
    i = pl.program_id(1)
    c = pl.program_id(2)
    u = u_ref[0]
    t = u.shape[0]
    halo = jnp.where(i > 0, halo_ref[0], 0.0)
    row8 = lax.broadcasted_iota(I32, halo.shape, 0)
    acc = u * w_ref[CONV_WIDTH - 1:CONV_WIDTH, :]
    head = u[0:8] * w_ref[CONV_WIDTH - 1:CONV_WIDTH, :]
    for s in range(1, CONV_WIDTH):
        wk = w_ref[CONV_WIDTH - 1 - s:CONV_WIDTH - s, :]
        rolled = pltpu.roll(u, s, 0)
        acc = acc + rolled * wk
        fixed = jnp.where(row8 < s, pltpu.roll(halo, s, 0), rolled[0:8])
        head = head + fixed * wk
    y = jnp.concatenate([head, acc[8:]], axis=0) if t > 8 else head
    y = _silu(y + b_ref[...])

    @pl.when(c < CONV_L2_BLOCKS)
    def _():
        scale = jnp.where(c < CONV_Q_BLOCKS, HEAD_DIM ** -0.5, 1.0)
        for k in range(CONV_COLS // HEAD_DIM):
            yk = y[:, k * HEAD_DIM:(k + 1) * HEAD_DIM]
            inv = lax.rsqrt(jnp.sum(yk * yk, axis=-1, keepdims=True) + NORM_EPS) * scale
            o_ref[0, :, k * HEAD_DIM:(k + 1) * HEAD_DIM] = yk * inv

    @pl.when(c >= CONV_L2_BLOCKS)
    def _():
        o_ref[0] = y


def _conv(p3, cw, cb):
    b, s, _ = p3.shape
    t = _tile(s, 1024)
    gdn0 = P_GQKV // CONV_COLS
    ssm_shift = P_SXBC // CONV_COLS - CONV_GDN_BLOCKS

    def col(c):
        return jnp.where(c < CONV_GDN_BLOCKS, c + gdn0, c + ssm_shift)

    return pl.pallas_call(
        _conv_kernel,
        out_shape=jax.ShapeDtypeStruct((b, s, C_WIDTH), F32),
        grid=(b, s // t, C_WIDTH // CONV_COLS),
        in_specs=[pl.BlockSpec((1, t, CONV_COLS), lambda bi, i, c: (bi, i, col(c))),
                  pl.BlockSpec((1, 8, CONV_COLS),
                               lambda bi, i, c: (bi, jnp.maximum(i * (t // 8) - 1, 0), col(c))),
                  pl.BlockSpec((CONV_WIDTH, CONV_COLS), lambda bi, i, c: (0, c)),
                  pl.BlockSpec((1, CONV_COLS), lambda bi, i, c: (0, c))],
        out_specs=pl.BlockSpec((1, t, CONV_COLS), lambda bi, i, c: (bi, i, c)),
        compiler_params=_params(("parallel", "parallel", "parallel")),
        name="conv_silu",
    )(p3, p3, cw, cb)


def _fox_kernel(qi_ref, kj_ref, q_ref, k_ref, v_ref, ck_ref, gn_ref, o_ref, m_ref, acc_ref):
    i = qi_ref[pl.program_id(1)]
    j = kj_ref[pl.program_id(1)]
    tq = q_ref.shape[1]
    tk = k_ref.shape[1]
    cols = lambda h: slice(h * HEAD_DIM, (h + 1) * HEAD_DIM)
    wide = lambda h: slice(2 * h * HEAD_DIM, 2 * (h + 1) * HEAD_DIM)

    @pl.when(j == 0)
    def _():
        m_ref[...] = jnp.full_like(m_ref, NEG_BIG)
        acc_ref[...] = jnp.zeros_like(acc_ref)

    def step(masked):
        if masked:
            causal = (lax.broadcasted_iota(I32, (tq, tk), 1) <= lax.broadcasted_iota(I32, (tq, tk), 0))
        ones = jnp.ones((tk, HEAD_DIM), BF16)
        s, p, alpha = {}, {}, {}

        def logits(h):
            q = (q_ref[0, :, cols(h)] * HEAD_DIM ** -0.5).astype(BF16)
            sh = _dot_nt(q, k_ref[0, :, cols(h)].astype(BF16)) - ck_ref[0, h:h + 1, :]
            s[h] = jnp.where(causal, sh, NEG_BIG) if masked else sh

        def softmax(h):
            m_prev = m_ref[h]
            m_new = jnp.maximum(m_prev, jnp.max(s[h], axis=-1, keepdims=True))
            p[h] = jnp.exp(s[h] - jnp.tile(m_new, (1, tk // LANES))).astype(BF16)
            alpha[h] = jnp.exp(m_prev - m_new)
            m_ref[h] = m_new

        def values(h):
            v1 = jnp.concatenate([v_ref[0, :, cols(h)].astype(BF16), ones], axis=1)
            acc_ref[:, wide(h)] = jnp.tile(alpha[h], (1, 2)) * acc_ref[:, wide(h)] + _dot(p[h], v1)

        for t in range(FOX_HEADS + 2):
            if t < FOX_HEADS:
                logits(t)
            if 0 <= t - 1 < FOX_HEADS:
                softmax(t - 1)
            if 0 <= t - 2 < FOX_HEADS:
                values(t - 2)

    @pl.when(j < i)
    def _():
        step(False)

    @pl.when(j == i)
    def _():
        step(True)
        for h in range(FOX_HEADS):
            both = acc_ref[:, wide(h)]
            o = both[:, :HEAD_DIM] / both[:, HEAD_DIM:]
            o_ref[0, :, cols(h)] = _rms(o, gn_ref[:, cols(h)]).astype(o_ref.dtype)


def _fox(p3, cst, gain):
    b, s, _ = p3.shape
    t = _tile(s, 512)
    n = s // t
    pairs = [(i, j) for i in range(n) for j in range(i + 1)]
    qi = jnp.asarray([p[0] for p in pairs], I32)
    kj = jnp.asarray([p[1] for p in pairs], I32)
    grid_spec = pltpu.PrefetchScalarGridSpec(
        num_scalar_prefetch=2,
        grid=(b, len(pairs)),
        in_specs=[pl.BlockSpec((1, t, FOX_W), lambda bi, p, qi, kj: (bi, qi[p], P_FQ // FOX_W)),
                  pl.BlockSpec((1, t, FOX_W), lambda bi, p, qi, kj: (bi, kj[p], P_FK // FOX_W)),
                  pl.BlockSpec((1, t, FOX_W), lambda bi, p, qi, kj: (bi, kj[p], P_FV // FOX_W)),
                  pl.BlockSpec((1, 8, t), lambda bi, p, qi, kj: (bi, 0, kj[p])),
                  pl.BlockSpec((1, FOX_W), lambda bi, p, qi, kj: (0, 0))],
        out_specs=pl.BlockSpec((1, t, FOX_W), lambda bi, p, qi, kj: (bi, qi[p], 0)),
        scratch_shapes=[pltpu.VMEM((FOX_HEADS, t, LANES), F32),
                        pltpu.VMEM((t, 2 * FOX_W), F32)],
    )
    return pl.pallas_call(
        _fox_kernel,
        out_shape=jax.ShapeDtypeStruct((b, s, FOX_W), BF16),
        grid_spec=grid_spec,
        compiler_params=_params(("parallel", "arbitrary")),
        name="fox_attention",
    )(qi, kj, p3, p3, p3, cst, gain)


def _gdn_kernel(q_ref, k_ref, v_ref, z_ref, val_ref, cs_ref, cst_ref, gn_ref, o_ref, state_ref):
    @pl.when(pl.program_id(1) == 0)
    def _():
        state_ref[...] = jnp.zeros_like(state_ref)

    c = GDN_CHUNK
    n_chunks = q_ref.shape[1] // c
    row = lax.broadcasted_iota(I32, (c, c), 0)
    col = lax.broadcasted_iota(I32, (c, c), 1)
    incl = row >= col
    strict = row > col
    eye = jnp.where(row == col, 1.0, 0.0)
    pairs = [(ci, h) for ci in range(n_chunks) for h in range(GDN_HEADS)]

    def rows(ci):
        return slice(ci * c, (ci + 1) * c)

    def cols(h):
        return slice(h * HEAD_DIM, (h + 1) * HEAD_DIM)

    q, k, kb, decay, eg, gc, rhs = {}, {}, {}, {}, {}, {}, {}
    for ci, h in pairs:
        g = (ci, h)
        q[g] = q_ref[0, rows(ci), cols(h)]
        k[g] = k_ref[0, rows(ci), cols(h)]
        beta = val_ref[0, rows(ci), S_GB + h:S_GB + h + 1]
        gc[g] = cs_ref[0, rows(ci), S_GA + h:S_GA + h + 1]
        gr = cst_ref[0, S_GA + h:S_GA + h + 1, rows(ci)]
        decay[g] = jnp.where(incl, jnp.exp(jnp.where(incl, gc[g] - gr, 0.0)), 0.0)
        eg[g] = jnp.exp(gc[g])
        kb[g] = k[g] * beta
        rhs[g] = jnp.concatenate([v_ref[0, rows(ci), cols(h)] * beta, kb[g] * eg[g]], axis=1)
    a = {g: jnp.where(strict, _dot_nt(kb[g], k[g]) * decay[g], 0.0) for g in pairs}
    qk = {g: jnp.where(incl, _dot_nt(q[g], k[g]) * decay[g], 0.0) for g in pairs}
    off = (row // 2 == col // 2) & (row != col)
    tinv = {g: eye - jnp.where(off, a[g], 0.0) for g in pairs}
    sz = 2
    while sz < c:
        off = (row // (2 * sz) == col // (2 * sz)) & (row // sz != col // sz)
        left = {g: _dot(tinv[g], jnp.where(off, a[g], 0.0)) for g in pairs}
        tinv = {g: tinv[g] - _dot(left[g], tinv[g]) for g in pairs}
        sz *= 2
    sol = {g: _dot(tinv[g], rhs[g]) for g in pairs}
    state = [state_ref[h] for h in range(GDN_HEADS)]
    heads = range(GDN_HEADS)
    for ci in range(n_chunks):
        ws = [_dot(sol[ci, h][:, HEAD_DIM:], state[h]) for h in heads]
        qs = [_dot(q[ci, h] * eg[ci, h], state[h]) for h in heads]
        v_new = [sol[ci, h][:, :HEAD_DIM] - ws[h] for h in heads]
        intra = [_dot(qk[ci, h], v_new[h]) for h in heads]
        for h in heads:
            g_last = gc[ci, h][c - 1:c, :]
            k_dec = k[ci, h] * jnp.exp(g_last - gc[ci, h])
            state[h] = state[h] * jnp.exp(g_last) + _dot_tn(k_dec, v_new[h])
        for h in heads:
            o = _rms(qs[h] + intra[h], gn_ref[...]) * _silu(z_ref[0, rows(ci), cols(h)])
            o_ref[0, rows(ci), cols(h)] = o.astype(o_ref.dtype)
    for h in heads:
        state_ref[h] = state[h]


def _gdn(cv, p3, val, cs, cst, gain):
    b, s, _ = cv.shape
    t = _tile(s, GDN_ROWS)
    wide = lambda blk: pl.BlockSpec((1, t, GDN_W), lambda bi, i: (bi, i, blk))
    small = pl.BlockSpec((1, t, LANES), lambda bi, i: (bi, i, 0))
    return pl.pallas_call(
        _gdn_kernel,
        out_shape=jax.ShapeDtypeStruct((b, s, GDN_W), BF16),
        grid=(b, s // t),
        in_specs=[wide(C_GQ // GDN_W), wide(C_GK // GDN_W), wide(C_GV // GDN_W),
                  wide(P_GZ // GDN_W), small, small,
                  pl.BlockSpec((1, LANES, t), lambda bi, i: (bi, 0, i)),
                  pl.BlockSpec((1, HEAD_DIM), lambda bi, i: (0, 0))],
        out_specs=pl.BlockSpec((1, t, GDN_W), lambda bi, i: (bi, i, 0)),
        scratch_shapes=[pltpu.VMEM((GDN_HEADS, HEAD_DIM, HEAD_DIM), F32)],
        compiler_params=_params(("parallel", "arbitrary")),
        name="gated_deltanet",
    )(cv, cv, cv, p3, val, cs, cst, gain)


SSM_PAIRS = SSM_HEADS // 2
PAIRS_PER_GROUP = SSM_PAIRS // SSM_GROUPS


def _ssd_kernel(x_ref, b_ref, c_ref, z_ref, val_ref, cs_ref, cst_ref, par_ref, o_ref, st_ref):
    @pl.when(pl.program_id(1) == 0)
    def _():
        st_ref[...] = jnp.zeros_like(st_ref)

    n = SSM_CHUNK
    row = lax.broadcasted_iota(I32, (n, n), 0)
    col = lax.broadcasted_iota(I32, (n, n), 1)
    incl = row >= col
    lo = lax.broadcasted_iota(I32, (n, LANES), 1) < SSM_HEAD_DIM
    lo_row = lo[0:1, :]
    groups = range(SSM_GROUPS)
    pairs = range(SSM_PAIRS)
    lanes_of = lambda p: slice(p * LANES, (p + 1) * LANES)
    grp = lambda p: p // PAIRS_PER_GROUP
    st = [st_ref[p] for p in pairs]

    for ci in range(x_ref.shape[1] // n):
        rs = slice(ci * n, (ci + 1) * n)

        def decay_of(h):
            ac = cs_ref[0, rs, S_DT + h:S_DT + h + 1]
            ar = cst_ref[0, S_DT + h:S_DT + h + 1, rs]
            return jnp.where(incl, jnp.exp(jnp.where(incl, ac - ar, 0.0)), 0.0), ac

        bg = [b_ref[0, rs, g * SSM_STATE:(g + 1) * SSM_STATE] for g in groups]
        cg = [c_ref[0, rs, g * SSM_STATE:(g + 1) * SSM_STATE] for g in groups]
        cb = [_dot_nt(cg[g], bg[g]) for g in groups]
        bg_t = [bg[g].T for g in groups]
        xp, xdt, dec, e_ac, tail, st_dec = [], [], [], [], [], []
        for p in pairs:
            h0, h1 = 2 * p, 2 * p + 1
            xp.append(x_ref[0, rs, lanes_of(p)])
            dt = jnp.where(lo, val_ref[0, rs, S_DT + h0:S_DT + h0 + 1],
                           val_ref[0, rs, S_DT + h1:S_DT + h1 + 1])
            xdt.append(xp[p] * dt)
            d0, ac0 = decay_of(h0)
            d1, ac1 = decay_of(h1)
            dec.append((d0, d1))
            e_ac.append(jnp.where(lo, jnp.exp(ac0), jnp.exp(ac1)))
            al0 = ac0[n - 1:n, :]
            al1 = ac1[n - 1:n, :]
            tail.append(jnp.where(lo, jnp.exp(al0 - ac0), jnp.exp(al1 - ac1)) * xdt[p])
            st_dec.append(jnp.where(lo_row, jnp.exp(al0), jnp.exp(al1)))
        diag0 = [_dot(cb[grp(p)] * dec[p][0], xdt[p]) for p in pairs]
        diag1 = [_dot(cb[grp(p)] * dec[p][1], xdt[p]) for p in pairs]
        y_off = [_dot(cg[grp(p)], st[p]) for p in pairs]
        st_add = [_dot(bg_t[grp(p)], tail[p]) for p in pairs]
        ys = []
        for p in pairs:
            st[p] = st[p] * st_dec[p] + st_add[p]
            y_diag = jnp.where(lo, diag0[p], diag1[p])
            ys.append((y_diag + y_off[p] * e_ac[p] + par_ref[0:1, lanes_of(p)] * xp[p])
                      * _silu(z_ref[0, rs, lanes_of(p)]))
        for g in groups:
            mine = [p for p in pairs if grp(p) == g]
            ssq = sum(jnp.sum(ys[p] * ys[p], axis=-1, keepdims=True) for p in mine)
            inv = lax.rsqrt(ssq / (PAIRS_PER_GROUP * LANES) + NORM_EPS)
            for p in mine:
                o_ref[0, rs, lanes_of(p)] = (ys[p] * inv * par_ref[1:2, lanes_of(p)]).astype(o_ref.dtype)
    for p in pairs:
        st_ref[p] = st[p]


def _ssd(cv, p3, val, cs, cst, par):
    b, s, _ = cv.shape
    t = _tile(s, SSD_ROWS)
    small = pl.BlockSpec((1, t, LANES), lambda bi, i: (bi, i, 0))
    return pl.pallas_call(
        _ssd_kernel,
        out_shape=jax.ShapeDtypeStruct((b, s, SSM_W), BF16),
        grid=(b, s // t),
        in_specs=[pl.BlockSpec((1, t, SSM_W), lambda bi, i: (bi, i, C_SX // SSM_W)),
                  pl.BlockSpec((1, t, BC_W), lambda bi, i: (bi, i, C_SB // BC_W)),
                  pl.BlockSpec((1, t, BC_W), lambda bi, i: (bi, i, C_SC // BC_W)),
                  pl.BlockSpec((1, t, SSM_W), lambda bi, i: (bi, i, P_SZ // SSM_W)),
                  small, small,
                  pl.BlockSpec((1, LANES, t), lambda bi, i: (bi, 0, i)),
                  pl.BlockSpec((8, SSM_W), lambda bi, i: (0, 0))],
        out_specs=pl.BlockSpec((1, t, SSM_W), lambda bi, i: (bi, i, 0)),
        scratch_shapes=[pltpu.VMEM((SSM_PAIRS, SSM_STATE, LANES), F32)],
        compiler_params=_params(("parallel", "arbitrary")),
        name="ssd_scan",
    )(cv, cv, cv, p3, val, cs, cst, par)


def _outproj_kernel(a1_ref, a2_ref, a3_ref, w_ref, x_ref, o_ref):
    r1 = a1_ref.shape[1]
    r2 = r1 + a2_ref.shape[1]
    o_ref[...] = (x_ref[...] + _dot(a1_ref[...], w_ref[0, :r1, :]) + _dot(a2_ref[...], w_ref[0, r1:r2, :])
                  + _dot(a3_ref[...], w_ref[0, r2:, :]))


def _outproj(a1, a2, a3, w, layer, x):
    n, d = x.shape
    tm = _tile(n, 1024)
    tn = _tile(d, 512)
    act = lambda a: pl.BlockSpec((tm, a.shape[1]), lambda i, j: (i, 0))
    return pl.pallas_call(
        _outproj_kernel,
        out_shape=jax.ShapeDtypeStruct((n, d), F32),
        grid=(n // tm, d // tn),
        in_specs=[act(a1), act(a2), act(a3),
                  pl.BlockSpec((1, w.shape[1], tn), lambda i, j: (layer, 0, j)),
                  pl.BlockSpec((tm, tn), lambda i, j: (i, j))],
        out_specs=pl.BlockSpec((tm, tn), lambda i, j: (i, j)),
        compiler_params=_params(("parallel", "parallel")),
        name="out_proj",
    )(a1, a2, a3, w, x)


def _ffn_kernel(x_ref, g_ref, wg_ref, wu_ref, wd_ref, o_ref, h_ref, acc_ref):
    f = pl.program_id(1)

    @pl.when(f == 0)
    def _():
        h_ref[...] = _rms(x_ref[...], g_ref[...]).astype(BF16)
        acc_ref[...] = jnp.zeros_like(acc_ref)

    h = h_ref[...]
    act = _silu(_dot(h, wg_ref[0])) * _dot(h, wu_ref[0])
    acc_ref[...] += _dot(act.astype(BF16), wd_ref[0])

    @pl.when(f == pl.num_programs(1) - 1)
    def _():
        o_ref[...] = x_ref[...] + acc_ref[...]


def _ffn(x, gain, wg, wu, wd, layer):
    n, d = x.shape
    f = wg.shape[2]
    tm = _tile(n, 512)
    tf = _tile(f, 512)
    return pl.pallas_call(
        _ffn_kernel,
        out_shape=jax.ShapeDtypeStruct((n, d), F32),
        grid=(n // tm, f // tf),
        in_specs=[pl.BlockSpec((tm, d), lambda i, j: (i, 0)),
                  pl.BlockSpec((1, d), lambda i, j: (0, 0)),
                  pl.BlockSpec((1, d, tf), lambda i, j: (layer, 0, j)),
                  pl.BlockSpec((1, d, tf), lambda i, j: (layer, 0, j)),
                  pl.BlockSpec((1, tf, d), lambda i, j: (layer, j, 0))],
        out_specs=pl.BlockSpec((tm, d), lambda i, j: (i, 0)),
        scratch_shapes=[pltpu.VMEM((tm, d), BF16), pltpu.VMEM((tm, d), F32)],
        compiler_params=_params(("parallel", "arbitrary")),
        name="dense_ffn",
    )(x, gain, wg, wu, wd)


def _router_kernel(x_ref, g_ref, r_ref, idx_ref, w_ref):
    h = _rms(x_ref[...], g_ref[...])
    logits = jnp.dot(h, r_ref[...], precision=lax.Precision.HIGHEST, preferred_element_type=F32)
    lane = lax.broadcasted_iota(I32, logits.shape, 1)
    l1 = jnp.where(lane < N_EXPERTS, logits, NEG_BIG)
    m1 = jnp.max(l1, axis=-1, keepdims=True)
    i1 = jnp.min(jnp.where(l1 == m1, lane, LANES), axis=-1, keepdims=True)
    l2 = jnp.where(lane == i1, NEG_BIG, l1)
    m2 = jnp.max(l2, axis=-1, keepdims=True)
    i2 = jnp.min(jnp.where(l2 == m2, lane, LANES), axis=-1, keepdims=True)
    e = jnp.exp(m2 - m1)
    idx_ref[...] = jnp.where(lane == 0, i1, jnp.where(lane == 1, i2, 0))
    w_ref[...] = jnp.where(lane == 0, 1.0 / (1.0 + e), jnp.where(lane == 1, e / (1.0 + e), 0.0))


def _router(x, gain, router_pad):
    n, d = x.shape
    tm = _tile(n, 512)
    return pl.pallas_call(
        _router_kernel,
        out_shape=(jax.ShapeDtypeStruct((n, LANES), I32), jax.ShapeDtypeStruct((n, LANES), F32)),
        grid=(n // tm,),
        in_specs=[pl.BlockSpec((tm, d), lambda i: (i, 0)),
                  pl.BlockSpec((1, d), lambda i: (0, 0)),
                  pl.BlockSpec((d, LANES), lambda i: (0, 0))],
        out_specs=(pl.BlockSpec((tm, LANES), lambda i: (i, 0)),
                   pl.BlockSpec((tm, LANES), lambda i: (i, 0))),
        compiler_params=_params(("parallel",)),
        name="moe_router",
    )(x, gain, router_pad)


def _row_copy(src_hbm, src_row, dst, dst_row, sem):
    return pltpu.make_async_copy(src_hbm.at[pl.ds(src_row, 1)], dst.at[pl.ds(dst_row, 1)], sem)


DMA_UNROLL = 8


def _gather_norm_kernel(nused_ref, tok_ref, tok_next_ref, x_hbm, g_ref, o_ref, buf, sems):
    i = pl.program_id(0)
    n_used = nused_ref[0]
    rows = o_ref.shape[0]
    slot = lax.rem(i, 2)

    def start_block(idx_ref, s):
        def body(r, carry):
            _row_copy(x_hbm, idx_ref[0, 0, r], buf.at[s], r, sems.at[s]).start()
            return carry
        lax.fori_loop(0, rows, body, 0, unroll=DMA_UNROLL)

    @pl.when((i == 0) & (n_used > 0))
    def _():
        start_block(tok_ref, 0)

    @pl.when(i + 1 < n_used)
    def _():
        start_block(tok_next_ref, 1 - slot)

    @pl.when(i < n_used)
    def _():
        def wait(r, carry):
            _row_copy(x_hbm, 0, buf.at[slot], r, sems.at[slot]).wait()
            return carry
        lax.fori_loop(0, rows, wait, 0, unroll=DMA_UNROLL)
        o_ref[...] = _rms(buf[slot], g_ref[...]).astype(o_ref.dtype)

    @pl.when(i >= n_used)
    def _():
        o_ref[...] = jnp.zeros_like(o_ref)


def _gather_norm(x, gain, slot_tok, n_used_blocks):
    n, d = x.shape
    n_slots = slot_tok.shape[0]
    rb = GATHER_ROWS
    nb = n_slots // rb
    tok3 = slot_tok.reshape(nb, 1, rb)
    grid_spec = pltpu.PrefetchScalarGridSpec(
        num_scalar_prefetch=1,
        grid=(nb,),
        in_specs=[pl.BlockSpec((1, 1, rb), lambda i, nu: (i, 0, 0), memory_space=pltpu.SMEM),
                  pl.BlockSpec((1, 1, rb), lambda i, nu: (jnp.minimum(i + 1, nb - 1), 0, 0),
                               memory_space=pltpu.SMEM),
                  pl.BlockSpec(memory_space=pl.ANY),
                  pl.BlockSpec((1, d), lambda i, nu: (0, 0))],
        out_specs=pl.BlockSpec((rb, d), lambda i, nu: (i, 0)),
        scratch_shapes=[pltpu.VMEM((2, rb, d), F32), pltpu.SemaphoreType.DMA((2,))],
    )
    return pl.pallas_call(
        _gather_norm_kernel,
        out_shape=jax.ShapeDtypeStruct((n_slots, d), BF16),
        grid_spec=grid_spec,
        compiler_params=_params(("arbitrary",)),
        name="moe_gather",
    )(n_used_blocks, tok3, tok3, x, gain)


def _expert_ffn_kernel(e_ref, na_ref, x_ref, wg_ref, wu_ref, wd_ref, o_ref):
    s = pl.program_id(0)
    f = pl.program_id(1)

    @pl.when(f == 0)
    def _():
        o_ref[...] = jnp.zeros_like(o_ref)

    @pl.when(s < na_ref[0])
    def _():
        x = x_ref[...]
        gate = _dot(x, wg_ref[0, 0].astype(BF16))
        up = _dot(x, wu_ref[0, 0].astype(BF16))
        o_ref[...] += _dot((_silu(gate) * up).astype(BF16), wd_ref[0, 0].astype(BF16))


def _expert_ffn(xs, blk_expert, n_active, wg, wu, wd, layer):
    n_slots, d = xs.shape
    f = wg.shape[3]
    nblk = n_slots // MOE_ROWS
    tf = _tile(f, 256)
    nf = f // tf

    def fidx(s, j, na_ref):
        return jnp.where(s < na_ref[0], j, nf - 1)

    grid_spec = pltpu.PrefetchScalarGridSpec(
        num_scalar_prefetch=2,
        grid=(nblk, nf),
        in_specs=[pl.BlockSpec((MOE_ROWS, d), lambda s, j, e, na: (s, 0)),
                  pl.BlockSpec((1, 1, d, tf), lambda s, j, e, na: (layer, e[s], 0, fidx(s, j, na))),
                  pl.BlockSpec((1, 1, d, tf), lambda s, j, e, na: (layer, e[s], 0, fidx(s, j, na))),
                  pl.BlockSpec((1, 1, tf, d), lambda s, j, e, na: (layer, e[s], fidx(s, j, na), 0))],
        out_specs=pl.BlockSpec((MOE_ROWS, d), lambda s, j, e, na: (s, 0)),
    )
    return pl.pallas_call(
        _expert_ffn_kernel,
        out_shape=jax.ShapeDtypeStruct((n_slots, d), F32),
        grid_spec=grid_spec,
        compiler_params=_params(("arbitrary", "arbitrary")),
        name="moe_expert_ffn",
    )(blk_expert, n_active, xs, wg, wu, wd)


def _combine_kernel(d0_ref, d1_ref, d0_next_ref, d1_next_ref, x_ref, w_ref, g_ref, y_hbm, o_ref,
                    buf, sems, *, final_norm):
    i = pl.program_id(0)
    rows = o_ref.shape[0]
    slot = lax.rem(i, 2)

    def start_block(a_ref, b_ref, s):
        def body(r, carry):
            _row_copy(y_hbm, a_ref[0, 0, r], buf.at[s, 0], r, sems.at[s]).start()
            _row_copy(y_hbm, b_ref[0, 0, r], buf.at[s, 1], r, sems.at[s]).start()
            return carry
        lax.fori_loop(0, rows, body, 0, unroll=DMA_UNROLL)

    @pl.when(i == 0)
    def _():
        start_block(d0_ref, d1_ref, 0)

    @pl.when(i + 1 < pl.num_programs(0))
    def _():
        start_block(d0_next_ref, d1_next_ref, 1 - slot)

    def wait(r, carry):
        _row_copy(y_hbm, 0, buf.at[slot, 0], r, sems.at[slot]).wait()
        _row_copy(y_hbm, 0, buf.at[slot, 1], r, sems.at[slot]).wait()
        return carry

    lax.fori_loop(0, rows, wait, 0, unroll=DMA_UNROLL)
    out = x_ref[...] + w_ref[:, 0:1] * buf[slot, 0] + w_ref[:, 1:2] * buf[slot, 1]
    o_ref[...] = _rms(out, g_ref[...]) if final_norm else out


def _combine(x, top_w, dest0, dest1, ys, final_gain=None):
    n, d = x.shape
    tb = _tile(n, 256)
    nb = n // tb
    dspec = pl.BlockSpec((1, 1, tb), lambda i: (i, 0, 0), memory_space=pltpu.SMEM)
    dnext = pl.BlockSpec((1, 1, tb), lambda i: (jnp.minimum(i + 1, nb - 1), 0, 0),
                         memory_space=pltpu.SMEM)
    d0 = dest0.reshape(nb, 1, tb)
    d1 = dest1.reshape(nb, 1, tb)
    gain = jnp.ones((1, d), F32) if final_gain is None else final_gain
    return pl.pallas_call(
        functools.partial(_combine_kernel, final_norm=final_gain is not None),
        out_shape=jax.ShapeDtypeStruct((n, d), F32),
        grid=(nb,),
        in_specs=[dspec, dspec, dnext, dnext,
                  pl.BlockSpec((tb, d), lambda i: (i, 0)),
                  pl.BlockSpec((tb, LANES), lambda i: (i, 0)),
                  pl.BlockSpec((1, d), lambda i: (0, 0)),
                  pl.BlockSpec(memory_space=pl.ANY)],
        out_specs=pl.BlockSpec((tb, d), lambda i: (i, 0)),
        scratch_shapes=[pltpu.VMEM((2, 2, tb, d), F32), pltpu.SemaphoreType.DMA((2,))],
        compiler_params=_params(("arbitrary",)),
        name="moe_combine",
    )(d0, d1, d0, d1, x, top_w, gain, ys)


def _moe(x, gain, router, wg, wu, wd, layer, final_gain=None):
    n, d = x.shape
    router_pad = jnp.pad(router, ((0, 0), (0, LANES - N_EXPERTS)))
    top_idx, top_w = _router(x, gain, router_pad)
    flat_e = top_idx[:, :TOP_K].reshape(-1)
    onehot = (flat_e[:, None] == jnp.arange(N_EXPERTS, dtype=I32)[None, :]).astype(I32)
    rank = jnp.take_along_axis(jnp.cumsum(onehot, axis=0) - onehot, flat_e[:, None], axis=1)[:, 0]
    counts = jnp.sum(onehot, axis=0)
    padded = (counts + MOE_ROWS - 1) // MOE_ROWS * MOE_ROWS
    pad_end = jnp.cumsum(padded)
    dest = (pad_end - padded)[flat_e] + rank
    nblk = (n * TOP_K) // MOE_ROWS + N_EXPERTS
    n_slots = nblk * MOE_ROWS
    slot_tok = jnp.zeros((n_slots,), I32).at[dest].set(jnp.arange(n * TOP_K, dtype=I32) // TOP_K)
    n_active = (pad_end[-1] // MOE_ROWS).astype(I32)
    blk_start = jnp.arange(nblk, dtype=I32) * MOE_ROWS
    blk_expert = jnp.minimum(jnp.searchsorted(pad_end, blk_start, side='right'), N_EXPERTS - 1)
    last_e = blk_expert[jnp.maximum(n_active - 1, 0)]
    blk_expert = jnp.where(jnp.arange(nblk) < n_active, blk_expert, last_e).astype(I32)

    xs = _gather_norm(x, gain, slot_tok, (n_active * (MOE_ROWS // GATHER_ROWS)).reshape(1))
    ys = _expert_ffn(xs, blk_expert, n_active.reshape(1), wg, wu, wd, layer)
    dest2 = dest.reshape(n, TOP_K)
    return _combine(x, top_w, dest2[:, 0], dest2[:, 1], ys, final_gain)


def _final_norm_kernel(x_ref, g_ref, o_ref):
    o_ref[...] = _rms(x_ref[...], g_ref[...])


def _final_norm(x, gain):
    n, d = x.shape
    tm = _tile(n, 512)
    return pl.pallas_call(
        _final_norm_kernel,
        out_shape=jax.ShapeDtypeStruct((n, d), F32),
        grid=(n // tm,),
        in_specs=[pl.BlockSpec((tm, d), lambda i: (i, 0)), pl.BlockSpec((1, d), lambda i: (0, 0))],
        out_specs=pl.BlockSpec((tm, d), lambda i: (i, 0)),
        compiler_params=_params(("parallel",)),
        name="final_norm",
    )(x, gain)


def _layout_w_in(w_in):
    offs = [0]
    for sz in PROJ_SIZES:
        offs.append(offs[-1] + sz)
    seg = [w_in[..., offs[k]:offs[k + 1]] for k in range(len(PROJ_SIZES))]
    fq, fk, fv, ff, gqkv, gz, gb, ga, sz_, sxbc, sdt = seg
    lead = w_in.shape[:-1]
    small = jnp.concatenate([ff, gb, ga, sdt, jnp.zeros(lead + (LANES - S_END,), w_in.dtype)], -1)
    tail = jnp.zeros(lead + (P_WIDTH - P_SMALL - LANES,), w_in.dtype)
    return jnp.concatenate([fq, fk, fv, gqkv, gz, sz_, sxbc, small, tail], -1).astype(BF16)


def _row_tile(rows, width):
    out = jnp.zeros((8, width), F32)
    for r, v in enumerate(rows):
        out = out.at[r, :v.shape[0]].set(v.astype(F32))
    return out


def kernel(x, norm_mix, w_in, fox_f_bias, fox_out_norm, gdn_conv_w, gdn_A_log, gdn_dt_bias,
           gdn_out_norm, ssm_conv_w, ssm_conv_b, ssm_A_log, ssm_dt_bias, ssm_D, ssm_out_norm,
           w_out, norm_ffn, ffn_w_gate, ffn_w_up, ffn_w_down, moe_router, moe_w_gate,
           moe_w_up, moe_w_down, norm_final):
    b, s, d = x.shape
    depth = w_in.shape[0]
    n = b * s
    w_in_l = _layout_w_in(w_in)
    w_out_b = w_out.astype(BF16)
    ffn_g, ffn_u, ffn_d = (w.astype(BF16) for w in (ffn_w_gate, ffn_w_up, ffn_w_down))
    conv_w = jnp.concatenate([gdn_conv_w, ssm_conv_w], axis=-1)
    conv_b = jnp.concatenate([jnp.zeros((depth, 3 * GDN_W), F32), ssm_conv_b], axis=-1)

    xf = x.reshape(n, d)
    for layer in range(depth):
        p = _norm_matmul(xf, norm_mix[layer][None, :], w_in_l, layer)
        p3 = p.reshape(b, s, P_WIDTH)
        zeros = lambda k: jnp.zeros((k,), F32)
        bias_row = jnp.concatenate([fox_f_bias[layer], zeros(GDN_HEADS), gdn_dt_bias[layer],
                                    ssm_dt_bias[layer]])
        alog_row = jnp.concatenate([zeros(S_GA), gdn_A_log[layer], ssm_A_log[layer]])
        val, cs, cst = _prep(p3, _row_tile([bias_row, alog_row], LANES))
        cv = _conv(p3, conv_w[layer], conv_b[layer][None, :])
        o_fox = _fox(p3, cst, fox_out_norm[layer].reshape(1, FOX_W))
        o_gdn = _gdn(cv, p3, val, cs, cst, gdn_out_norm[layer][None, :])
        ssm_par = _row_tile([jnp.repeat(ssm_D[layer], SSM_HEAD_DIM), ssm_out_norm[layer]], SSM_W)
        o_ssm = _ssd(cv, p3, val, cs, cst, ssm_par)
        xf = _outproj(o_fox.reshape(n, FOX_W), o_gdn.reshape(n, GDN_W), o_ssm.reshape(n, SSM_W),
                      w_out_b, layer, xf)
        j = layer // 2
        gain = norm_ffn[layer][None, :]
        if layer % 2 == 0:
            xf = _ffn(xf, gain, ffn_g, ffn_u, ffn_d, j)
        else:
            fused_final = norm_final[None, :] if layer == depth - 1 else None
            xf = _moe(xf, gain, moe_router[j], moe_w_gate, moe_w_up, moe_w_down, j, fused_final)
    if depth % 2:
        xf = _final_norm(xf, norm_final[None, :])
    return xf.reshape(b, s, d)
```

```python
import functools

import jax
import jax.numpy as jnp
from jax import lax
from jax.experimental import pallas as pl
from jax.experimental.pallas import tpu as pltpu

F32 = jnp.float32
BF16 = jnp.bfloat16
I32 = jnp.int32

NORM_EPS = 1e-6
LANES = 128
NEG_BIG = -1e30

FOX_HEADS = 4
HEAD_DIM = 128
GDN_HEADS = 6
GDN_CHUNK = 64
GDN_ROWS = 256
SSM_HEADS = 12
SSM_HEAD_DIM = 64
SSM_STATE = 128
SSM_GROUPS = 2
SSM_CHUNK = 128
SSD_ROWS = 256
PREP_ROWS = 512
CONV_WIDTH = 4
N_EXPERTS = 8
TOP_K = 2

FOX_W = FOX_HEADS * HEAD_DIM
GDN_W = GDN_HEADS * HEAD_DIM
SSM_W = SSM_HEADS * SSM_HEAD_DIM
BC_W = SSM_GROUPS * SSM_STATE
PROJ_SIZES = (FOX_W, FOX_W, FOX_W, FOX_HEADS, 3 * GDN_W, GDN_W, GDN_HEADS, GDN_HEADS,
              SSM_W, SSM_W + 2 * BC_W, SSM_HEADS)

P_FQ, P_FK, P_FV = 0, FOX_W, 2 * FOX_W
P_GQKV = 3 * FOX_W
P_GZ = P_GQKV + 3 * GDN_W
P_SZ = P_GZ + GDN_W
P_SXBC = P_SZ + SSM_W
P_SMALL = P_SXBC + SSM_W + 2 * BC_W
P_WIDTH = 6912
S_FF, S_GB, S_GA, S_DT = 0, 4, 10, 16
S_END = S_DT + SSM_HEADS
C_GQ, C_GK, C_GV = 0, GDN_W, 2 * GDN_W
C_SX = 3 * GDN_W
C_SB = C_SX + SSM_W
C_SC = C_SB + BC_W
C_WIDTH = C_SC + BC_W

MOE_ROWS = 1024
VMEM_LIMIT = 56 * 1024 * 1024


def _tile(n, pref):
    t = min(n, pref)
    while n % t:
        t //= 2
    return t


def _params(sem, vmem=VMEM_LIMIT):
    return pltpu.CompilerParams(dimension_semantics=sem, vmem_limit_bytes=vmem)


def _silu(x):
    return x / (1.0 + jnp.exp(-x))


def _softplus(x):
    return jnp.maximum(x, 0.0) + jnp.log1p(jnp.exp(-jnp.abs(x)))


def _rms(x, gain):
    return x * lax.rsqrt(jnp.mean(x * x, axis=-1, keepdims=True) + NORM_EPS) * gain


def _dot(a, b):
    return jnp.dot(a, b, preferred_element_type=F32)


def _dot_nt(a, b):
    return lax.dot_general(a, b, (((1,), (1,)), ((), ())), preferred_element_type=F32)


def _dot_tn(a, b):
    return lax.dot_general(a, b, (((0,), (0,)), ((), ())), preferred_element_type=F32)


def _norm_matmul_kernel(x_ref, g_ref, w_ref, o_ref, h_ref):
    @pl.when(pl.program_id(1) == 0)
    def _():
        h_ref[...] = _rms(x_ref[...], g_ref[...]).astype(BF16)

    o_ref[...] = _dot(h_ref[...], w_ref[0])


def _norm_matmul(x, gain, w, layer):
    n, d = x.shape
    nout = w.shape[2]
    tm = _tile(n, 1024)
    tn = _tile(nout, 768)
    return pl.pallas_call(
        _norm_matmul_kernel,
        out_shape=jax.ShapeDtypeStruct((n, nout), F32),
        grid=(n // tm, nout // tn),
        in_specs=[pl.BlockSpec((tm, d), lambda i, j: (i, 0)),
                  pl.BlockSpec((1, d), lambda i, j: (0, 0)),
                  pl.BlockSpec((1, d, tn), lambda i, j: (layer, 0, j))],
        out_specs=pl.BlockSpec((tm, tn), lambda i, j: (i, j)),
        scratch_shapes=[pltpu.VMEM((tm, d), BF16)],
        compiler_params=_params(("parallel", "arbitrary")),
        name="norm_inproj",
    )(x, gain, w)


def _prep_kernel(p_ref, par_ref, val_ref, cs_ref, cst_ref, carry_ref):
    @pl.when(pl.program_id(1) == 0)
    def _():
        carry_ref[...] = jnp.zeros_like(carry_ref)

    blk = SSM_CHUNK
    lane = lax.broadcasted_iota(I32, (blk, LANES), 1)
    row = lax.broadcasted_iota(I32, (blk, blk), 0)
    col = lax.broadcasted_iota(I32, (blk, blk), 1)
    tri = row >= col
    tri_blk = jnp.where(tri, 1.0, 0.0)
    tri_gdn = jnp.where(tri & (row // GDN_CHUNK == col // GDN_CHUNK), 1.0, 0.0)
    neg_a = -jnp.exp(par_ref[1:2, :])
    carry = carry_ref[...]
    for sb in range(p_ref.shape[1] // blk):
        rs = slice(sb * blk, (sb + 1) * blk)
        v = p_ref[0, rs, :] + par_ref[0:1, :]
        sp = _softplus(v)
        log_f = -_softplus(-v)
        beta = 1.0 / (1.0 + jnp.exp(-v))
        val_ref[0, rs, :] = jnp.where(lane < S_GA, beta, sp)
        z = jnp.where(lane < S_GB, log_f, jnp.where(lane < S_GA, 0.0, neg_a * sp))
        z = jnp.where(lane < S_END, z, 0.0)
        cs_blk = jnp.dot(tri_blk, z, precision=lax.Precision.HIGHEST, preferred_element_type=F32)
        cs_gdn = jnp.dot(tri_gdn, z, precision=lax.Precision.HIGHEST, preferred_element_type=F32)
        cs_run = cs_blk + carry
        carry = cs_run[blk - 1:blk, :]
        cs = jnp.where(lane < S_GB, cs_run, jnp.where(lane < S_DT, cs_gdn, cs_blk))
        cs_ref[0, rs, :] = cs
        cst_ref[0, :, rs] = cs.T
    carry_ref[...] = carry


def _prep(p3, par):
    b, s, _ = p3.shape
    blk = _tile(s, PREP_ROWS)
    shp = jax.ShapeDtypeStruct((b, s, LANES), F32)
    return pl.pallas_call(
        _prep_kernel,
        out_shape=(shp, shp, jax.ShapeDtypeStruct((b, LANES, s), F32)),
        grid=(b, s // blk),
        in_specs=[pl.BlockSpec((1, blk, LANES), lambda bi, i: (bi, i, P_SMALL // LANES)),
                  pl.BlockSpec((8, LANES), lambda bi, i: (0, 0))],
        out_specs=(pl.BlockSpec((1, blk, LANES), lambda bi, i: (bi, i, 0)),
                   pl.BlockSpec((1, blk, LANES), lambda bi, i: (bi, i, 0)),
                   pl.BlockSpec((1, LANES, blk), lambda bi, i: (bi, 0, i))),
        scratch_shapes=[pltpu.VMEM((1, LANES), F32)],
        compiler_params=_params(("parallel", "arbitrary")),
        name="gate_prep",
    )(p3, par)


CONV_COLS = 256
CONV_GDN_BLOCKS = 3 * GDN_W // CONV_COLS
CONV_L2_BLOCKS = 2 * GDN_W // CONV_COLS
CONV_Q_BLOCKS = GDN_W // CONV_COLS


CONV_STRIP = 64


def _conv_kernel(u_ref, halo_ref, w_ref, b_ref, o_ref, edge_ref):
    i = pl.program_id(1)
    c = pl.program_id(2)
    t = u_ref.shape[1]
    strip = min(CONV_STRIP, t)
    edge_ref[0:8, :] = jnp.where(i > 0, halo_ref[0], 0.0)
    edge_ref[8:, :] = u_ref[0, 0:strip, :]
    taps = [w_ref[k:k + 1, :] for k in range(CONV_WIDTH)]
    bias = b_ref[...]

    def conv_strip(r0):
        acc = bias
        for k in range(CONV_WIDTH):
            back = CONV_WIDTH - 1 - k
            if r0 == 0:
                rows = edge_ref[8 - back:8 - back + strip, :]
            else:
                rows = u_ref[0, r0 - back:r0 - back + strip, :]
            acc = acc + rows * taps[k]
        return _silu(acc)

    @pl.when(c < CONV_L2_BLOCKS)
    def _():
        scale = jnp.where(c < CONV_Q_BLOCKS, HEAD_DIM ** -0.5, 1.0)
        for r0 in range(0, t, strip):
            y = conv_strip(r0)
            for k in range(CONV_COLS // HEAD_DIM):
                yk = y[:, k * HEAD_DIM:(k + 1) * HEAD_DIM]
                inv = lax.rsqrt(jnp.sum(yk * yk, axis=-1, keepdims=True) + NORM_EPS) * scale
                o_ref[0, r0:r0 + strip, k * HEAD_DIM:(k + 1) * HEAD_DIM] = yk * inv

    @pl.when(c >= CONV_L2_BLOCKS)
    def _():
        for r0 in range(0, t, strip):
            o_ref[0, r0:r0 + strip, :] = conv_strip(r0)


def _conv(p3, cw, cb):
    b, s, _ = p3.shape
    t = _tile(s, 1024)
    gdn0 = P_GQKV // CONV_COLS
    ssm_shift = P_SXBC // CONV_COLS - CONV_GDN_BLOCKS

    def col(c):
        return jnp.where(c < CONV_GDN_BLOCKS, c + gdn0, c + ssm_shift)

    return pl.pallas_call(
        _conv_kernel,
        out_shape=jax.ShapeDtypeStruct((b, s, C_WIDTH), F32),
        grid=(b, s // t, C_WIDTH // CONV_COLS),
        in_specs=[pl.BlockSpec((1, t, CONV_COLS), lambda bi, i, c: (bi, i, col(c))),
                  pl.BlockSpec((1, 8, CONV_COLS),
                               lambda bi, i, c: (bi, jnp.maximum(i * (t // 8) - 1, 0), col(c))),
                  pl.BlockSpec((CONV_WIDTH, CONV_COLS), lambda bi, i, c: (0, c)),
                  pl.BlockSpec((1, CONV_COLS), lambda bi, i, c: (0, c))],
        out_specs=pl.BlockSpec((1, t, CONV_COLS), lambda bi, i, c: (bi, i, c)),
        scratch_shapes=[pltpu.VMEM((8 + min(CONV_STRIP, t), CONV_COLS), F32)],
        compiler_params=_params(("parallel", "parallel", "parallel")),
        name="conv_silu",
    )(p3, p3, cw, cb)


def _fox_kernel(qi_ref, kj_ref, q_ref, k_ref, v_ref, ck_ref, gn_ref, o_ref, m_ref, acc_ref):
    i = qi_ref[pl.program_id(1)]
    j = kj_ref[pl.program_id(1)]
    tq = q_ref.shape[1]
    tk = k_ref.shape[1]
    cols = lambda h: slice(h * HEAD_DIM, (h + 1) * HEAD_DIM)
    wide = lambda h: slice(2 * h * HEAD_DIM, 2 * (h + 1) * HEAD_DIM)

    @pl.when(j == 0)
    def _():
        m_ref[...] = jnp.full_like(m_ref, NEG_BIG)
        acc_ref[...] = jnp.zeros_like(acc_ref)

    def step(masked):
        if masked:
            causal = (lax.broadcasted_iota(I32, (tq, tk), 1) <= lax.broadcasted_iota(I32, (tq, tk), 0))
        ones = jnp.ones((tk, HEAD_DIM), BF16)
        s, p, alpha = {}, {}, {}

        def logits(h):
            q = (q_ref[0, :, cols(h)] * HEAD_DIM ** -0.5).astype(BF16)
            sh = _dot_nt(q, k_ref[0, :, cols(h)].astype(BF16)) - ck_ref[0, h:h + 1, :]
            s[h] = jnp.where(causal, sh, NEG_BIG) if masked else sh

        def softmax(h):
            m_prev = m_ref[h]
            m_new = jnp.maximum(m_prev, jnp.max(s[h], axis=-1, keepdims=True))
            p[h] = jnp.exp(s[h] - jnp.tile(m_new, (1, tk // LANES))).astype(BF16)
            alpha[h] = jnp.exp(m_prev - m_new)
            m_ref[h] = m_new

        def values(h):
            v1 = jnp.concatenate([v_ref[0, :, cols(h)].astype(BF16), ones], axis=1)
            acc_ref[:, wide(h)] = jnp.tile(alpha[h], (1, 2)) * acc_ref[:, wide(h)] + _dot(p[h], v1)

        for t in range(FOX_HEADS + 2):
            if t < FOX_HEADS:
                logits(t)
            if 0 <= t - 1 < FOX_HEADS:
                softmax(t - 1)
            if 0 <= t - 2 < FOX_HEADS:
                values(t - 2)

    @pl.when(j < i)
    def _():
        step(False)

    @pl.when(j == i)
    def _():
        step(True)
        for h in range(FOX_HEADS):
            both = acc_ref[:, wide(h)]
            o = both[:, :HEAD_DIM] / both[:, HEAD_DIM:]
            o_ref[0, :, cols(h)] = _rms(o, gn_ref[:, cols(h)]).astype(o_ref.dtype)


def _fox(p3, cst, gain):
    b, s, _ = p3.shape
    t = _tile(s, 512)
    n = s // t
    pairs = [(i, j) for i in range(n) for j in range(i + 1)]
    qi = jnp.asarray([p[0] for p in pairs], I32)
    kj = jnp.asarray([p[1] for p in pairs], I32)
    grid_spec = pltpu.PrefetchScalarGridSpec(
        num_scalar_prefetch=2,
        grid=(b, len(pairs)),
        in_specs=[pl.BlockSpec((1, t, FOX_W), lambda bi, p, qi, kj: (bi, qi[p], P_FQ // FOX_W)),
                  pl.BlockSpec((1, t, FOX_W), lambda bi, p, qi, kj: (bi, kj[p], P_FK // FOX_W)),
                  pl.BlockSpec((1, t, FOX_W), lambda bi, p, qi, kj: (bi, kj[p], P_FV // FOX_W)),
                  pl.BlockSpec((1, 8, t), lambda bi, p, qi, kj: (bi, 0, kj[p])),
                  pl.BlockSpec((1, FOX_W), lambda bi, p, qi, kj: (0, 0))],
        out_specs=pl.BlockSpec((1, t, FOX_W), lambda bi, p, qi, kj: (bi, qi[p], 0)),
        scratch_shapes=[pltpu.VMEM((FOX_HEADS, t, LANES), F32),
                        pltpu.VMEM((t, 2 * FOX_W), F32)],
    )
    return pl.pallas_call(
        _fox_kernel,
        out_shape=jax.ShapeDtypeStruct((b, s, FOX_W), BF16),
        grid_spec=grid_spec,
        compiler_params=_params(("parallel", "arbitrary")),
        name="fox_attention",
    )(qi, kj, p3, p3, p3, cst, gain)


def _gdn_kernel(q_ref, k_ref, v_ref, z_ref, val_ref, cs_ref, cst_ref, gn_ref, o_ref, state_ref):
    @pl.when(pl.program_id(1) == 0)
    def _():
        state_ref[...] = jnp.zeros_like(state_ref)

    c = GDN_CHUNK
    n_chunks = q_ref.shape[1] // c
    row = lax.broadcasted_iota(I32, (c, c), 0)
    col = lax.broadcasted_iota(I32, (c, c), 1)
    incl = row >= col
    strict = row > col
    eye = jnp.where(row == col, 1.0, 0.0)
    pairs = [(ci, h) for ci in range(n_chunks) for h in range(GDN_HEADS)]

    def rows(ci):
        return slice(ci * c, (ci + 1) * c)

    def cols(h):
        return slice(h * HEAD_DIM, (h + 1) * HEAD_DIM)

    q, k, kb, decay, eg, gc, rhs = {}, {}, {}, {}, {}, {}, {}
    for ci, h in pairs:
        g = (ci, h)
        q[g] = q_ref[0, rows(ci), cols(h)]
        k[g] = k_ref[0, rows(ci), cols(h)]
        beta = val_ref[0, rows(ci), S_GB + h:S_GB + h + 1]
        gc[g] = cs_ref[0, rows(ci), S_GA + h:S_GA + h + 1]
        gr = cst_ref[0, S_GA + h:S_GA + h + 1, rows(ci)]
        decay[g] = jnp.where(incl, jnp.exp(jnp.where(incl, gc[g] - gr, 0.0)), 0.0)
        eg[g] = jnp.exp(gc[g])
        kb[g] = k[g] * beta
        rhs[g] = jnp.concatenate([v_ref[0, rows(ci), cols(h)] * beta, kb[g] * eg[g]], axis=1)
    a = {g: jnp.where(strict, _dot_nt(kb[g], k[g]) * decay[g], 0.0) for g in pairs}
    qk = {g: jnp.where(incl, _dot_nt(q[g], k[g]) * decay[g], 0.0) for g in pairs}
    off = (row // 2 == col // 2) & (row != col)
    tinv = {g: eye - jnp.where(off, a[g], 0.0) for g in pairs}
    sz = 2
    while sz < c:
        off = (row // (2 * sz) == col // (2 * sz)) & (row // sz != col // sz)
        left = {g: _dot(tinv[g], jnp.where(off, a[g], 0.0)) for g in pairs}
        tinv = {g: tinv[g] - _dot(left[g], tinv[g]) for g in pairs}
        sz *= 2
    sol = {g: _dot(tinv[g], rhs[g]) for g in pairs}
    state = [state_ref[h] for h in range(GDN_HEADS)]
    heads = range(GDN_HEADS)
    for ci in range(n_chunks):
        ws = [_dot(sol[ci, h][:, HEAD_DIM:], state[h]) for h in heads]
        qs = [_dot(q[ci, h] * eg[ci, h], state[h]) for h in heads]
        v_new = [sol[ci, h][:, :HEAD_DIM] - ws[h] for h in heads]
        intra = [_dot(qk[ci, h], v_new[h]) for h in heads]
        for h in heads:
            g_last = gc[ci, h][c - 1:c, :]
            k_dec = k[ci, h] * jnp.exp(g_last - gc[ci, h])
            state[h] = state[h] * jnp.exp(g_last) + _dot_tn(k_dec, v_new[h])
        for h in heads:
            o = _rms(qs[h] + intra[h], gn_ref[...]) * _silu(z_ref[0, rows(ci), cols(h)])
            o_ref[0, rows(ci), cols(h)] = o.astype(o_ref.dtype)
    for h in heads:
        state_ref[h] = state[h]


def _gdn(cv, p3, val, cs, cst, gain):
    b, s, _ = cv.shape
    t = _tile(s, GDN_ROWS)
    wide = lambda blk: pl.BlockSpec((1, t, GDN_W), lambda bi, i: (bi, i, blk))
    small = pl.BlockSpec((1, t, LANES), lambda bi, i: (bi, i, 0))
    return pl.pallas_call(
        _gdn_kernel,
        out_shape=jax.ShapeDtypeStruct((b, s, GDN_W), BF16),
        grid=(b, s // t),
        in_specs=[wide(C_GQ // GDN_W), wide(C_GK // GDN_W), wide(C_GV // GDN_W),
                  wide(P_GZ // GDN_W), small, small,
                  pl.BlockSpec((1, LANES, t), lambda bi, i: (bi, 0, i)),
                  pl.BlockSpec((1, HEAD_DIM), lambda bi, i: (0, 0))],
        out_specs=pl.BlockSpec((1, t, GDN_W), lambda bi, i: (bi, i, 0)),
        scratch_shapes=[pltpu.VMEM((GDN_HEADS, HEAD_DIM, HEAD_DIM), F32)],
        compiler_params=_params(("parallel", "arbitrary")),
        name="gated_deltanet",
    )(cv, cv, cv, p3, val, cs, cst, gain)


SSM_PAIRS = SSM_HEADS // 2
PAIRS_PER_GROUP = SSM_PAIRS // SSM_GROUPS


def _ssd_kernel(x_ref, b_ref, c_ref, z_ref, val_ref, cs_ref, cst_ref, par_ref, o_ref, st_ref):
    @pl.when(pl.program_id(1) == 0)
    def _():
        st_ref[...] = jnp.zeros_like(st_ref)

    n = SSM_CHUNK
    row = lax.broadcasted_iota(I32, (n, n), 0)
    col = lax.broadcasted_iota(I32, (n, n), 1)
    incl = row >= col
    lo = lax.broadcasted_iota(I32, (n, LANES), 1) < SSM_HEAD_DIM
    lo_row = lo[0:1, :]
    groups = range(SSM_GROUPS)
    pairs = range(SSM_PAIRS)
    lanes_of = lambda p: slice(p * LANES, (p + 1) * LANES)
    grp = lambda p: p // PAIRS_PER_GROUP
    st = [st_ref[p] for p in pairs]

    for ci in range(x_ref.shape[1] // n):
        rs = slice(ci * n, (ci + 1) * n)

        def decay_of(h):
            ac = cs_ref[0, rs, S_DT + h:S_DT + h + 1]
            ar = cst_ref[0, S_DT + h:S_DT + h + 1, rs]
            return jnp.where(incl, jnp.exp(jnp.where(incl, ac - ar, 0.0)), 0.0), ac

        bg = [b_ref[0, rs, g * SSM_STATE:(g + 1) * SSM_STATE] for g in groups]
        cg = [c_ref[0, rs, g * SSM_STATE:(g + 1) * SSM_STATE] for g in groups]
        cb = [_dot_nt(cg[g], bg[g]) for g in groups]
        bg_t = [bg[g].T for g in groups]
        xp, xdt, dec, e_ac, tail, st_dec = [], [], [], [], [], []
        for p in pairs:
            h0, h1 = 2 * p, 2 * p + 1
            xp.append(x_ref[0, rs, lanes_of(p)])
            dt = jnp.where(lo, val_ref[0, rs, S_DT + h0:S_DT + h0 + 1],
                           val_ref[0, rs, S_DT + h1:S_DT + h1 + 1])
            xdt.append(xp[p] * dt)
            d0, ac0 = decay_of(h0)
            d1, ac1 = decay_of(h1)
            dec.append((d0, d1))
            e_ac.append(jnp.where(lo, jnp.exp(ac0), jnp.exp(ac1)))
            al0 = ac0[n - 1:n, :]
            al1 = ac1[n - 1:n, :]
            tail.append(jnp.where(lo, jnp.exp(al0 - ac0), jnp.exp(al1 - ac1)) * xdt[p])
            st_dec.append(jnp.where(lo_row, jnp.exp(al0), jnp.exp(al1)))
        diag0 = [_dot(cb[grp(p)] * dec[p][0], xdt[p]) for p in pairs]
        diag1 = [_dot(cb[grp(p)] * dec[p][1], xdt[p]) for p in pairs]
        y_off = [_dot(cg[grp(p)], st[p]) for p in pairs]
        st_add = [_dot(bg_t[grp(p)], tail[p]) for p in pairs]
        ys = []
        for p in pairs:
            st[p] = st[p] * st_dec[p] + st_add[p]
            y_diag = jnp.where(lo, diag0[p], diag1[p])
            ys.append((y_diag + y_off[p] * e_ac[p] + par_ref[0:1, lanes_of(p)] * xp[p])
                      * _silu(z_ref[0, rs, lanes_of(p)]))
        for g in groups:
            mine = [p for p in pairs if grp(p) == g]
            ssq = sum(jnp.sum(ys[p] * ys[p], axis=-1, keepdims=True) for p in mine)
            inv = lax.rsqrt(ssq / (PAIRS_PER_GROUP * LANES) + NORM_EPS)
            for p in mine:
                o_ref[0, rs, lanes_of(p)] = (ys[p] * inv * par_ref[1:2, lanes_of(p)]).astype(o_ref.dtype)
    for p in pairs:
        st_ref[p] = st[p]


def _ssd(cv, p3, val, cs, cst, par):
    b, s, _ = cv.shape
    t = _tile(s, SSD_ROWS)
    small = pl.BlockSpec((1, t, LANES), lambda bi, i: (bi, i, 0))
    return pl.pallas_call(
        _ssd_kernel,
        out_shape=jax.ShapeDtypeStruct((b, s, SSM_W), BF16),
        grid=(b, s // t),
        in_specs=[pl.BlockSpec((1, t, SSM_W), lambda bi, i: (bi, i, C_SX // SSM_W)),
                  pl.BlockSpec((1, t, BC_W), lambda bi, i: (bi, i, C_SB // BC_W)),
                  pl.BlockSpec((1, t, BC_W), lambda bi, i: (bi, i, C_SC // BC_W)),
                  pl.BlockSpec((1, t, SSM_W), lambda bi, i: (bi, i, P_SZ // SSM_W)),
                  small, small,
                  pl.BlockSpec((1, LANES, t), lambda bi, i: (bi, 0, i)),
                  pl.BlockSpec((8, SSM_W), lambda bi, i: (0, 0))],
        out_specs=pl.BlockSpec((1, t, SSM_W), lambda bi, i: (bi, i, 0)),
        scratch_shapes=[pltpu.VMEM((SSM_PAIRS, SSM_STATE, LANES), F32)],
        compiler_params=_params(("parallel", "arbitrary")),
        name="ssd_scan",
    )(cv, cv, cv, p3, val, cs, cst, par)


def _outproj_kernel(a1_ref, a2_ref, a3_ref, w_ref, x_ref, o_ref):
    r1 = a1_ref.shape[1]
    r2 = r1 + a2_ref.shape[1]
    o_ref[...] = (x_ref[...] + _dot(a1_ref[...], w_ref[0, :r1, :]) + _dot(a2_ref[...], w_ref[0, r1:r2, :])
                  + _dot(a3_ref[...], w_ref[0, r2:, :]))


def _outproj(a1, a2, a3, w, layer, x):
    n, d = x.shape
    tm = _tile(n, 1024)
    tn = _tile(d, 512)
    act = lambda a: pl.BlockSpec((tm, a.shape[1]), lambda i, j: (i, 0))
    return pl.pallas_call(
        _outproj_kernel,
        out_shape=jax.ShapeDtypeStruct((n, d), F32),
        grid=(n // tm, d // tn),
        in_specs=[act(a1), act(a2), act(a3),
                  pl.BlockSpec((1, w.shape[1], tn), lambda i, j: (layer, 0, j)),
                  pl.BlockSpec((tm, tn), lambda i, j: (i, j))],
        out_specs=pl.BlockSpec((tm, tn), lambda i, j: (i, j)),
        compiler_params=_params(("parallel", "parallel")),
        name="out_proj",
    )(a1, a2, a3, w, x)


def _ffn_kernel(x_ref, g_ref, wg_ref, wu_ref, wd_ref, o_ref, h_ref, acc_ref):
    f = pl.program_id(1)

    @pl.when(f == 0)
    def _():
        h_ref[...] = _rms(x_ref[...], g_ref[...]).astype(BF16)
        acc_ref[...] = jnp.zeros_like(acc_ref)

    h = h_ref[...]
    act = _silu(_dot(h, wg_ref[0])) * _dot(h, wu_ref[0])
    acc_ref[...] += _dot(act.astype(BF16), wd_ref[0])

    @pl.when(f == pl.num_programs(1) - 1)
    def _():
        o_ref[...] = x_ref[...] + acc_ref[...]


def _ffn(x, gain, wg, wu, wd, layer):
    n, d = x.shape
    f = wg.shape[2]
    tm = _tile(n, 512)
    tf = _tile(f, 512)
    return pl.pallas_call(
        _ffn_kernel,
        out_shape=jax.ShapeDtypeStruct((n, d), F32),
        grid=(n // tm, f // tf),
        in_specs=[pl.BlockSpec((tm, d), lambda i, j: (i, 0)),
                  pl.BlockSpec((1, d), lambda i, j: (0, 0)),
                  pl.BlockSpec((1, d, tf), lambda i, j: (layer, 0, j)),
                  pl.BlockSpec((1, d, tf), lambda i, j: (layer, 0, j)),
                  pl.BlockSpec((1, tf, d), lambda i, j: (layer, j, 0))],
        out_specs=pl.BlockSpec((tm, d), lambda i, j: (i, 0)),
        scratch_shapes=[pltpu.VMEM((tm, d), BF16), pltpu.VMEM((tm, d), F32)],
        compiler_params=_params(("parallel", "arbitrary")),
        name="dense_ffn",
    )(x, gain, wg, wu, wd)


def _router_kernel(x_ref, g_ref, r_ref, idx_ref, w_ref):
    h = _rms(x_ref[...], g_ref[...])
    r = r_ref[...]
    h_hi = h.astype(BF16)
    r_hi = r.astype(BF16)
    h_lo = (h - h_hi.astype(F32)).astype(BF16)
    r_lo = (r - r_hi.astype(F32)).astype(BF16)
    logits = _dot(h_hi, r_hi) + (_dot(h_hi, r_lo) + _dot(h_lo, r_hi))
    lane = lax.broadcasted_iota(I32, logits.shape, 1)
    l1 = jnp.where(lane < N_EXPERTS, logits, NEG_BIG)
    m1 = jnp.max(l1, axis=-1, keepdims=True)
    i1 = jnp.min(jnp.where(l1 == m1, lane, LANES), axis=-1, keepdims=True)
    l2 = jnp.where(lane == i1, NEG_BIG, l1)
    m2 = jnp.max(l2, axis=-1, keepdims=True)
    i2 = jnp.min(jnp.where(l2 == m2, lane, LANES), axis=-1, keepdims=True)
    e = jnp.exp(m2 - m1)
    idx_ref[...] = jnp.where(lane == 0, i1, jnp.where(lane == 1, i2, 0))
    w_ref[...] = jnp.where(lane == 0, 1.0 / (1.0 + e), jnp.where(lane == 1, e / (1.0 + e), 0.0))


def _router(x, gain, router_pad):
    n, d = x.shape
    tm = _tile(n, 512)
    return pl.pallas_call(
        _router_kernel,
        out_shape=(jax.ShapeDtypeStruct((n, LANES), I32), jax.ShapeDtypeStruct((n, LANES), F32)),
        grid=(n // tm,),
        in_specs=[pl.BlockSpec((tm, d), lambda i: (i, 0)),
                  pl.BlockSpec((1, d), lambda i: (0, 0)),
                  pl.BlockSpec((d, LANES), lambda i: (0, 0))],
        out_specs=(pl.BlockSpec((tm, LANES), lambda i: (i, 0)),
                   pl.BlockSpec((tm, LANES), lambda i: (i, 0))),
        compiler_params=_params(("parallel",)),
        name="moe_router",
    )(x, gain, router_pad)


def _row_copy(src_hbm, src_row, dst, dst_row, sem):
    return pltpu.make_async_copy(src_hbm.at[pl.ds(src_row, 1)], dst.at[pl.ds(dst_row, 1)], sem)


DMA_UNROLL = 8


def _dispatch_kernel(d0_ref, d1_ref, x_ref, g_ref, zeros_hbm, xs_hbm, hbuf, sems):
    del zeros_hbm
    i = pl.program_id(0)
    last = pl.num_programs(0) - 1
    rows = x_ref.shape[0]
    slot = lax.rem(i, 2)

    def wait_block(s):
        def body(r, carry):
            _row_copy(hbuf.at[s], r, xs_hbm, 0, sems.at[s]).wait()
            _row_copy(hbuf.at[s], r, xs_hbm, 0, sems.at[s]).wait()
            return carry
        lax.fori_loop(0, rows, body, 0, unroll=DMA_UNROLL)

    @pl.when(i >= 2)
    def _():
        wait_block(slot)

    hbuf[slot] = _rms(x_ref[...], g_ref[...])

    def start(r, carry):
        _row_copy(hbuf.at[slot], r, xs_hbm, d0_ref[0, 0, r], sems.at[slot]).start()
        _row_copy(hbuf.at[slot], r, xs_hbm, d1_ref[0, 0, r], sems.at[slot]).start()
        return carry

    lax.fori_loop(0, rows, start, 0, unroll=DMA_UNROLL)

    @pl.when(i == last)
    def _():
        wait_block(slot)

    @pl.when((i == last) & (i >= 1))
    def _():
        wait_block(1 - slot)


def _dispatch(x, gain, dest0, dest1, n_slots):
    n, d = x.shape
    tb = _tile(n, 256)
    nb = n // tb
    dspec = pl.BlockSpec((1, 1, tb), lambda i: (i, 0, 0), memory_space=pltpu.SMEM)
    return pl.pallas_call(
        _dispatch_kernel,
        out_shape=jax.ShapeDtypeStruct((n_slots, d), F32),
        grid=(nb,),
        in_specs=[dspec, dspec,
                  pl.BlockSpec((tb, d), lambda i: (i, 0)),
                  pl.BlockSpec((1, d), lambda i: (0, 0)),
                  pl.BlockSpec(memory_space=pl.ANY)],
        out_specs=pl.BlockSpec(memory_space=pl.ANY),
        scratch_shapes=[pltpu.VMEM((2, tb, d), F32), pltpu.SemaphoreType.DMA((2,))],
        input_output_aliases={4: 0},
        compiler_params=_params(("arbitrary",)),
        name="moe_dispatch",
    )(dest0.reshape(nb, 1, tb), dest1.reshape(nb, 1, tb), x, gain, jnp.zeros((n_slots, d), F32))


def _expert_ffn_kernel(e_ref, na_ref, x_ref, wg_ref, wu_ref, wd_ref, o_ref, xb_ref):
    s = pl.program_id(0)
    f = pl.program_id(1)

    @pl.when(f == 0)
    def _():
        o_ref[...] = jnp.zeros_like(o_ref)
        xb_ref[...] = x_ref[...].astype(BF16)

    @pl.when(s < na_ref[0])
    def _():
        x = xb_ref[...]
        gate = _dot(x, wg_ref[0, 0].astype(BF16))
        up = _dot(x, wu_ref[0, 0].astype(BF16))
        o_ref[...] += _dot((_silu(gate) * up).astype(BF16), wd_ref[0, 0].astype(BF16))


def _expert_ffn(xs, blk_expert, n_active, wg, wu, wd, layer):
    n_slots, d = xs.shape
    f = wg.shape[3]
    nblk = n_slots // MOE_ROWS
    tf = _tile(f, 256)
    nf = f // tf

    def fidx(s, j, na_ref):
        return jnp.where(s < na_ref[0], j, nf - 1)

    grid_spec = pltpu.PrefetchScalarGridSpec(
        num_scalar_prefetch=2,
        grid=(nblk, nf),
        in_specs=[pl.BlockSpec((MOE_ROWS, d), lambda s, j, e, na: (s, 0)),
                  pl.BlockSpec((1, 1, d, tf), lambda s, j, e, na: (layer, e[s], 0, fidx(s, j, na))),
                  pl.BlockSpec((1, 1, d, tf), lambda s, j, e, na: (layer, e[s], 0, fidx(s, j, na))),
                  pl.BlockSpec((1, 1, tf, d), lambda s, j, e, na: (layer, e[s], fidx(s, j, na), 0))],
        out_specs=pl.BlockSpec((MOE_ROWS, d), lambda s, j, e, na: (s, 0)),
        scratch_shapes=[pltpu.VMEM((MOE_ROWS, d), BF16)],
    )
    return pl.pallas_call(
        _expert_ffn_kernel,
        out_shape=jax.ShapeDtypeStruct((n_slots, d), F32),
        grid_spec=grid_spec,
        compiler_params=_params(("arbitrary", "arbitrary")),
        name="moe_expert_ffn",
    )(blk_expert, n_active, xs, wg, wu, wd)


def _combine_kernel(d0_ref, d1_ref, d0_next_ref, d1_next_ref, x_ref, w_ref, g_ref, y_hbm, o_ref,
                    buf, sems, *, final_norm):
    i = pl.program_id(0)
    rows = o_ref.shape[0]
    slot = lax.rem(i, 2)

    def start_block(a_ref, b_ref, s):
        def body(r, carry):
            _row_copy(y_hbm, a_ref[0, 0, r], buf.at[s, 0], r, sems.at[s]).start()
            _row_copy(y_hbm, b_ref[0, 0, r], buf.at[s, 1], r, sems.at[s]).start()
            return carry
        lax.fori_loop(0, rows, body, 0, unroll=DMA_UNROLL)

    @pl.when(i == 0)
    def _():
        start_block(d0_ref, d1_ref, 0)

    @pl.when(i + 1 < pl.num_programs(0))
    def _():
        start_block(d0_next_ref, d1_next_ref, 1 - slot)

    def wait(r, carry):
        _row_copy(y_hbm, 0, buf.at[slot, 0], r, sems.at[slot]).wait()
        _row_copy(y_hbm, 0, buf.at[slot, 1], r, sems.at[slot]).wait()
        return carry

    lax.fori_loop(0, rows, wait, 0, unroll=DMA_UNROLL)
    out = x_ref[...] + w_ref[:, 0:1] * buf[slot, 0] + w_ref[:, 1:2] * buf[slot, 1]
    o_ref[...] = _rms(out, g_ref[...]) if final_norm else out


def _combine(x, top_w, dest0, dest1, ys, final_gain=None):
    n, d = x.shape
    tb = _tile(n, 256)
    nb = n // tb
    dspec = pl.BlockSpec((1, 1, tb), lambda i: (i, 0, 0), memory_space=pltpu.SMEM)
    dnext = pl.BlockSpec((1, 1, tb), lambda i: (jnp.minimum(i + 1, nb - 1), 0, 0),
                         memory_space=pltpu.SMEM)
    d0 = dest0.reshape(nb, 1, tb)
    d1 = dest1.reshape(nb, 1, tb)
    gain = jnp.ones((1, d), F32) if final_gain is None else final_gain
    return pl.pallas_call(
        functools.partial(_combine_kernel, final_norm=final_gain is not None),
        out_shape=jax.ShapeDtypeStruct((n, d), F32),
        grid=(nb,),
        in_specs=[dspec, dspec, dnext, dnext,
                  pl.BlockSpec((tb, d), lambda i: (i, 0)),
                  pl.BlockSpec((tb, LANES), lambda i: (i, 0)),
                  pl.BlockSpec((1, d), lambda i: (0, 0)),
                  pl.BlockSpec(memory_space=pl.ANY)],
        out_specs=pl.BlockSpec((tb, d), lambda i: (i, 0)),
        scratch_shapes=[pltpu.VMEM((2, 2, tb, d), F32), pltpu.SemaphoreType.DMA((2,))],
        compiler_params=_params(("arbitrary",)),
        name="moe_combine",
    )(d0, d1, d0, d1, x, top_w, gain, ys)


def _moe(x, gain, router, wg, wu, wd, layer, final_gain=None):
    n, d = x.shape
    router_pad = jnp.pad(router, ((0, 0), (0, LANES - N_EXPERTS)))
    top_idx, top_w = _router(x, gain, router_pad)
    flat_e = top_idx[:, :TOP_K].reshape(-1)
    onehot = (flat_e[:, None] == jnp.arange(N_EXPERTS, dtype=I32)[None, :]).astype(I32)
    rank = jnp.take_along_axis(jnp.cumsum(onehot, axis=0) - onehot, flat_e[:, None], axis=1)[:, 0]
    counts = jnp.sum(onehot, axis=0)
    padded = (counts + MOE_ROWS - 1) // MOE_ROWS * MOE_ROWS
    pad_end = jnp.cumsum(padded)
    dest = (pad_end - padded)[flat_e] + rank
    nblk = (n * TOP_K) // MOE_ROWS + N_EXPERTS
    n_slots = nblk * MOE_ROWS
    n_active = (pad_end[-1] // MOE_ROWS).astype(I32)
    blk_start = jnp.arange(nblk, dtype=I32) * MOE_ROWS
    blk_expert = jnp.minimum(jnp.searchsorted(pad_end, blk_start, side='right'), N_EXPERTS - 1)
    last_e = blk_expert[jnp.maximum(n_active - 1, 0)]
    blk_expert = jnp.where(jnp.arange(nblk) < n_active, blk_expert, last_e).astype(I32)

    dest2 = dest.reshape(n, TOP_K)
    xs = _dispatch(x, gain, dest2[:, 0], dest2[:, 1], n_slots)
    ys = _expert_ffn(xs, blk_expert, n_active.reshape(1), wg, wu, wd, layer)
    return _combine(x, top_w, dest2[:, 0], dest2[:, 1], ys, final_gain)


def _final_norm_kernel(x_ref, g_ref, o_ref):
    o_ref[...] = _rms(x_ref[...], g_ref[...])


def _final_norm(x, gain):
    n, d = x.shape
    tm = _tile(n, 512)
    return pl.pallas_call(
        _final_norm_kernel,
        out_shape=jax.ShapeDtypeStruct((n, d), F32),
        grid=(n // tm,),
        in_specs=[pl.BlockSpec((tm, d), lambda i: (i, 0)), pl.BlockSpec((1, d), lambda i: (0, 0))],
        out_specs=pl.BlockSpec((tm, d), lambda i: (i, 0)),
        compiler_params=_params(("parallel",)),
        name="final_norm",
    )(x, gain)


def _layout_w_in(w_in):
    offs = [0]
    for sz in PROJ_SIZES:
        offs.append(offs[-1] + sz)
    seg = [w_in[..., offs[k]:offs[k + 1]] for k in range(len(PROJ_SIZES))]
    fq, fk, fv, ff, gqkv, gz, gb, ga, sz_, sxbc, sdt = seg
    lead = w_in.shape[:-1]
    small = jnp.concatenate([ff, gb, ga, sdt, jnp.zeros(lead + (LANES - S_END,), w_in.dtype)], -1)
    tail = jnp.zeros(lead + (P_WIDTH - P_SMALL - LANES,), w_in.dtype)
    return jnp.concatenate([fq, fk, fv, gqkv, gz, sz_, sxbc, small, tail], -1).astype(BF16)


def _row_tile(rows, width):
    out = jnp.zeros((8, width), F32)
    for r, v in enumerate(rows):
        out = out.at[r, :v.shape[0]].set(v.astype(F32))
    return out


def kernel(x, norm_mix, w_in, fox_f_bias, fox_out_norm, gdn_conv_w, gdn_A_log, gdn_dt_bias,
           gdn_out_norm, ssm_conv_w, ssm_conv_b, ssm_A_log, ssm_dt_bias, ssm_D, ssm_out_norm,
           w_out, norm_ffn, ffn_w_gate, ffn_w_up, ffn_w_down, moe_router, moe_w_gate,
           moe_w_up, moe_w_down, norm_final):
    b, s, d = x.shape
    depth = w_in.shape[0]
    n = b * s
    w_in_l = _layout_w_in(w_in)
    w_out_b = w_out.astype(BF16)
    ffn_g, ffn_u, ffn_d = (w.astype(BF16) for w in (ffn_w_gate, ffn_w_up, ffn_w_down))
    conv_w = jnp.concatenate([gdn_conv_w, ssm_conv_w], axis=-1)
    conv_b = jnp.concatenate([jnp.zeros((depth, 3 * GDN_W), F32), ssm_conv_b], axis=-1)

    xf = x.reshape(n, d)
    for layer in range(depth):
        p = _norm_matmul(xf, norm_mix[layer][None, :], w_in_l, layer)
        p3 = p.reshape(b, s, P_WIDTH)
        zeros = lambda k: jnp.zeros((k,), F32)
        bias_row = jnp.concatenate([fox_f_bias[layer], zeros(GDN_HEADS), gdn_dt_bias[layer],
                                    ssm_dt_bias[layer]])
        alog_row = jnp.concatenate([zeros(S_GA), gdn_A_log[layer], ssm_A_log[layer]])
        val, cs, cst = _prep(p3, _row_tile([bias_row, alog_row], LANES))
        cv = _conv(p3, conv_w[layer], conv_b[layer][None, :])
        o_fox = _fox(p3, cst, fox_out_norm[layer].reshape(1, FOX_W))
        o_gdn = _gdn(cv, p3, val, cs, cst, gdn_out_norm[layer][None, :])
        ssm_par = _row_tile([jnp.repeat(ssm_D[layer], SSM_HEAD_DIM), ssm_out_norm[layer]], SSM_W)
        o_ssm = _ssd(cv, p3, val, cs, cst, ssm_par)
        xf = _outproj(o_fox.reshape(n, FOX_W), o_gdn.reshape(n, GDN_W), o_ssm.reshape(n, SSM_W),
                      w_out_b, layer, xf)
        j = layer // 2
        gain = norm_ffn[layer][None, :]
        if layer % 2 == 0:
            xf = _ffn(xf, gain, ffn_g, ffn_u, ffn_d, j)
        else:
            fused_final = norm_final[None, :] if layer == depth - 1 else None
            xf = _moe(xf, gain, moe_router[j], moe_w_gate, moe_w_up, moe_w_down, j, fused_final)
    if depth % 2:
        xf = _final_norm(xf, norm_final[None, :])
    return xf.reshape(b, s, d)
```

```python
import functools

import jax
import jax.numpy as jnp
from jax import lax
from jax.experimental import pallas as pl
from jax.experimental.pallas import tpu as pltpu

F32 = jnp.float32
BF16 = jnp.bfloat16
I32 = jnp.int32

NORM_EPS = 1e-6
LANES = 128
NEG_BIG = -1e30

FOX_HEADS = 4
HEAD_DIM = 128
GDN_HEADS = 6
GDN_CHUNK = 64
GDN_ROWS = 256
SSM_HEADS = 12
SSM_HEAD_DIM = 64
SSM_STATE = 128
SSM_GROUPS = 2
SSM_CHUNK = 128
SSD_ROWS = 256
PREP_ROWS = 512
CONV_WIDTH = 4
N_EXPERTS = 8
TOP_K = 2

FOX_W = FOX_HEADS * HEAD_DIM
GDN_W = GDN_HEADS * HEAD_DIM
SSM_W = SSM_HEADS * SSM_HEAD_DIM
BC_W = SSM_GROUPS * SSM_STATE
PROJ_SIZES = (FOX_W, FOX_W, FOX_W, FOX_HEADS, 3 * GDN_W, GDN_W, GDN_HEADS, GDN_HEADS,
              SSM_W, SSM_W + 2 * BC_W, SSM_HEADS)

P_FQ, P_FK, P_FV = 0, FOX_W, 2 * FOX_W
P_GQKV = 3 * FOX_W
P_GZ = P_GQKV + 3 * GDN_W
P_SZ = P_GZ + GDN_W
P_SXBC = P_SZ + SSM_W
P_SMALL = P_SXBC + SSM_W + 2 * BC_W
P_WIDTH = 6912
S_FF, S_GB, S_GA, S_DT = 0, 4, 10, 16
S_END = S_DT + SSM_HEADS
C_GQ, C_GK, C_GV = 0, GDN_W, 2 * GDN_W
C_SX = 3 * GDN_W
C_SB = C_SX + SSM_W
C_SC = C_SB + BC_W
C_WIDTH = C_SC + BC_W

MOE_ROWS = 1024
VMEM_LIMIT = 56 * 1024 * 1024


def _tile(n, pref):
    t = min(n, pref)
    while n % t:
        t //= 2
    return t


def _params(sem, vmem=VMEM_LIMIT):
    return pltpu.CompilerParams(dimension_semantics=sem, vmem_limit_bytes=vmem)


def _silu(x):
    return x / (1.0 + jnp.exp(-x))


def _softplus(x):
    return jnp.maximum(x, 0.0) + jnp.log1p(jnp.exp(-jnp.abs(x)))


def _rms(x, gain):
    return x * lax.rsqrt(jnp.mean(x * x, axis=-1, keepdims=True) + NORM_EPS) * gain


def _dot(a, b):
    return jnp.dot(a, b, preferred_element_type=F32)


def _dot_nt(a, b):
    return lax.dot_general(a, b, (((1,), (1,)), ((), ())), preferred_element_type=F32)


def _dot_tn(a, b):
    return lax.dot_general(a, b, (((0,), (0,)), ((), ())), preferred_element_type=F32)


def _norm_matmul_kernel(x_ref, g_ref, w_ref, o_ref, h_ref):
    @pl.when(pl.program_id(1) == 0)
    def _():
        h_ref[...] = _rms(x_ref[...], g_ref[...]).astype(BF16)

    o_ref[...] = _dot(h_ref[...], w_ref[0])


def _norm_matmul(x, gain, w, layer):
    n, d = x.shape
    nout = w.shape[2]
    tm = _tile(n, 1024)
    tn = _tile(nout, 1152)
    return pl.pallas_call(
        _norm_matmul_kernel,
        out_shape=jax.ShapeDtypeStruct((n, nout), F32),
        grid=(n // tm, nout // tn),
        in_specs=[pl.BlockSpec((tm, d), lambda i, j: (i, 0)),
                  pl.BlockSpec((1, d), lambda i, j: (0, 0)),
                  pl.BlockSpec((1, d, tn), lambda i, j: (layer, 0, j))],
        out_specs=pl.BlockSpec((tm, tn), lambda i, j: (i, j)),
        scratch_shapes=[pltpu.VMEM((tm, d), BF16)],
        compiler_params=_params(("parallel", "arbitrary")),
        name="norm_inproj",
    )(x, gain, w)


def _prep_kernel(p_ref, par_ref, val_ref, cs_ref, cst_ref, carry_ref):
    @pl.when(pl.program_id(1) == 0)
    def _():
        carry_ref[...] = jnp.zeros_like(carry_ref)

    blk = SSM_CHUNK
    lane = lax.broadcasted_iota(I32, (blk, LANES), 1)
    row = lax.broadcasted_iota(I32, (blk, blk), 0)
    col = lax.broadcasted_iota(I32, (blk, blk), 1)
    tri = row >= col
    tri_blk = jnp.where(tri, 1.0, 0.0)
    tri_gdn = jnp.where(tri & (row // GDN_CHUNK == col // GDN_CHUNK), 1.0, 0.0)
    neg_a = -jnp.exp(par_ref[1:2, :])
    carry = carry_ref[...]
    for sb in range(p_ref.shape[1] // blk):
        rs = slice(sb * blk, (sb + 1) * blk)
        v = p_ref[0, rs, :] + par_ref[0:1, :]
        sp = _softplus(v)
        log_f = -_softplus(-v)
        beta = 1.0 / (1.0 + jnp.exp(-v))
        val_ref[0, rs, :] = jnp.where(lane < S_GA, beta, sp)
        z = jnp.where(lane < S_GB, log_f, jnp.where(lane < S_GA, 0.0, neg_a * sp))
        z = jnp.where(lane < S_END, z, 0.0)
        cs_blk = jnp.dot(tri_blk, z, precision=lax.Precision.HIGHEST, preferred_element_type=F32)
        cs_gdn = jnp.dot(tri_gdn, z, precision=lax.Precision.HIGHEST, preferred_element_type=F32)
        cs_run = cs_blk + carry
        carry = cs_run[blk - 1:blk, :]
        cs = jnp.where(lane < S_GB, cs_run, jnp.where(lane < S_DT, cs_gdn, cs_blk))
        cs_ref[0, rs, :] = cs
        cst_ref[0, :, rs] = cs.T
    carry_ref[...] = carry


def _prep(p3, par):
    b, s, _ = p3.shape
    blk = _tile(s, PREP_ROWS)
    shp = jax.ShapeDtypeStruct((b, s, LANES), F32)
    return pl.pallas_call(
        _prep_kernel,
        out_shape=(shp, shp, jax.ShapeDtypeStruct((b, LANES, s), F32)),
        grid=(b, s // blk),
        in_specs=[pl.BlockSpec((1, blk, LANES), lambda bi, i: (bi, i, P_SMALL // LANES)),
                  pl.BlockSpec((8, LANES), lambda bi, i: (0, 0))],
        out_specs=(pl.BlockSpec((1, blk, LANES), lambda bi, i: (bi, i, 0)),
                   pl.BlockSpec((1, blk, LANES), lambda bi, i: (bi, i, 0)),
                   pl.BlockSpec((1, LANES, blk), lambda bi, i: (bi, 0, i))),
        scratch_shapes=[pltpu.VMEM((1, LANES), F32)],
        compiler_params=_params(("parallel", "arbitrary")),
        name="gate_prep",
    )(p3, par)


CONV_COLS = 256
CONV_GDN_BLOCKS = 3 * GDN_W // CONV_COLS
CONV_L2_BLOCKS = 2 * GDN_W // CONV_COLS
CONV_Q_BLOCKS = GDN_W // CONV_COLS


CONV_STRIP = 64


def _conv_kernel(u_ref, halo_ref, w_ref, b_ref, o_ref, edge_ref):
    i = pl.program_id(1)
    c = pl.program_id(2)
    t = u_ref.shape[1]
    strip = min(CONV_STRIP, t)
    edge_ref[0:8, :] = jnp.where(i > 0, halo_ref[0], 0.0)
    edge_ref[8:, :] = u_ref[0, 0:strip, :]
    taps = [w_ref[k:k + 1, :] for k in range(CONV_WIDTH)]
    bias = b_ref[...]

    def conv_strip(r0):
        acc = bias
        for k in range(CONV_WIDTH):
            back = CONV_WIDTH - 1 - k
            if r0 == 0:
                rows = edge_ref[8 - back:8 - back + strip, :]
            else:
                rows = u_ref[0, r0 - back:r0 - back + strip, :]
            acc = acc + rows * taps[k]
        return _silu(acc)

    @pl.when(c < CONV_L2_BLOCKS)
    def _():
        scale = jnp.where(c < CONV_Q_BLOCKS, HEAD_DIM ** -0.5, 1.0)
        for r0 in range(0, t, strip):
            y = conv_strip(r0)
            for k in range(CONV_COLS // HEAD_DIM):
                yk = y[:, k * HEAD_DIM:(k + 1) * HEAD_DIM]
                inv = lax.rsqrt(jnp.sum(yk * yk, axis=-1, keepdims=True) + NORM_EPS) * scale
                o_ref[0, r0:r0 + strip, k * HEAD_DIM:(k + 1) * HEAD_DIM] = yk * inv

    @pl.when(c >= CONV_L2_BLOCKS)
    def _():
        for r0 in range(0, t, strip):
            o_ref[0, r0:r0 + strip, :] = conv_strip(r0)


def _conv(p3, cw, cb):
    b, s, _ = p3.shape
    t = _tile(s, 1024)
    gdn0 = P_GQKV // CONV_COLS
    ssm_shift = P_SXBC // CONV_COLS - CONV_GDN_BLOCKS

    def col(c):
        return jnp.where(c < CONV_GDN_BLOCKS, c + gdn0, c + ssm_shift)

    return pl.pallas_call(
        _conv_kernel,
        out_shape=jax.ShapeDtypeStruct((b, s, C_WIDTH), F32),
        grid=(b, s // t, C_WIDTH // CONV_COLS),
        in_specs=[pl.BlockSpec((1, t, CONV_COLS), lambda bi, i, c: (bi, i, col(c))),
                  pl.BlockSpec((1, 8, CONV_COLS),
                               lambda bi, i, c: (bi, jnp.maximum(i * (t // 8) - 1, 0), col(c))),
                  pl.BlockSpec((CONV_WIDTH, CONV_COLS), lambda bi, i, c: (0, c)),
                  pl.BlockSpec((1, CONV_COLS), lambda bi, i, c: (0, c))],
        out_specs=pl.BlockSpec((1, t, CONV_COLS), lambda bi, i, c: (bi, i, c)),
        scratch_shapes=[pltpu.VMEM((8 + min(CONV_STRIP, t), CONV_COLS), F32)],
        compiler_params=_params(("parallel", "parallel", "parallel")),
        name="conv_silu",
    )(p3, p3, cw, cb)


def _fox_kernel(qi_ref, kj_ref, q_ref, k_ref, v_ref, ck_ref, gn_ref, o_ref, m_ref, acc_ref):
    i = qi_ref[pl.program_id(1)]
    j = kj_ref[pl.program_id(1)]
    tq = q_ref.shape[1]
    tk = k_ref.shape[1]
    cols = lambda h: slice(h * HEAD_DIM, (h + 1) * HEAD_DIM)
    wide = lambda h: slice(2 * h * HEAD_DIM, 2 * (h + 1) * HEAD_DIM)

    @pl.when(j == 0)
    def _():
        m_ref[...] = jnp.full_like(m_ref, NEG_BIG)
        acc_ref[...] = jnp.zeros_like(acc_ref)

    def step(masked):
        if masked:
            causal = (lax.broadcasted_iota(I32, (tq, tk), 1) <= lax.broadcasted_iota(I32, (tq, tk), 0))
        ones = jnp.ones((tk, HEAD_DIM), BF16)
        s, p, alpha = {}, {}, {}

        def logits(h):
            q = (q_ref[0, :, cols(h)] * HEAD_DIM ** -0.5).astype(BF16)
            sh = _dot_nt(q, k_ref[0, :, cols(h)].astype(BF16)) - ck_ref[0, h:h + 1, :]
            s[h] = jnp.where(causal, sh, NEG_BIG) if masked else sh

        def softmax(h):
            m_prev = m_ref[h]
            m_new = jnp.maximum(m_prev, jnp.max(s[h], axis=-1, keepdims=True))
            p[h] = jnp.exp(s[h] - jnp.tile(m_new, (1, tk // LANES))).astype(BF16)
            alpha[h] = jnp.exp(m_prev - m_new)
            m_ref[h] = m_new

        def values(h):
            v1 = jnp.concatenate([v_ref[0, :, cols(h)].astype(BF16), ones], axis=1)
            acc_ref[:, wide(h)] = jnp.tile(alpha[h], (1, 2)) * acc_ref[:, wide(h)] + _dot(p[h], v1)

        for t in range(FOX_HEADS + 2):
            if t < FOX_HEADS:
                logits(t)
            if 0 <= t - 1 < FOX_HEADS:
                softmax(t - 1)
            if 0 <= t - 2 < FOX_HEADS:
                values(t - 2)

    @pl.when(j < i)
    def _():
        step(False)

    @pl.when(j == i)
    def _():
        step(True)
        for h in range(FOX_HEADS):
            both = acc_ref[:, wide(h)]
            o = both[:, :HEAD_DIM] / both[:, HEAD_DIM:]
            o_ref[0, :, cols(h)] = _rms(o, gn_ref[:, cols(h)]).astype(o_ref.dtype)


def _fox(p3, cst, gain):
    b, s, _ = p3.shape
    t = _tile(s, 512)
    n = s // t
    pairs = [(i, j) for i in range(n) for j in range(i + 1)]
    qi = jnp.asarray([p[0] for p in pairs], I32)
    kj = jnp.asarray([p[1] for p in pairs], I32)
    grid_spec = pltpu.PrefetchScalarGridSpec(
        num_scalar_prefetch=2,
        grid=(b, len(pairs)),
        in_specs=[pl.BlockSpec((1, t, FOX_W), lambda bi, p, qi, kj: (bi, qi[p], P_FQ // FOX_W)),
                  pl.BlockSpec((1, t, FOX_W), lambda bi, p, qi, kj: (bi, kj[p], P_FK // FOX_W)),
                  pl.BlockSpec((1, t, FOX_W), lambda bi, p, qi, kj: (bi, kj[p], P_FV // FOX_W)),
                  pl.BlockSpec((1, 8, t), lambda bi, p, qi, kj: (bi, 0, kj[p])),
                  pl.BlockSpec((1, FOX_W), lambda bi, p, qi, kj: (0, 0))],
        out_specs=pl.BlockSpec((1, t, FOX_W), lambda bi, p, qi, kj: (bi, qi[p], 0)),
        scratch_shapes=[pltpu.VMEM((FOX_HEADS, t, LANES), F32),
                        pltpu.VMEM((t, 2 * FOX_W), F32)],
    )
    return pl.pallas_call(
        _fox_kernel,
        out_shape=jax.ShapeDtypeStruct((b, s, FOX_W), BF16),
        grid_spec=grid_spec,
        compiler_params=_params(("parallel", "arbitrary")),
        name="fox_attention",
    )(qi, kj, p3, p3, p3, cst, gain)


def _gdn_kernel(q_ref, k_ref, v_ref, z_ref, val_ref, cs_ref, cst_ref, gn_ref, o_ref, state_ref):
    @pl.when(pl.program_id(1) == 0)
    def _():
        state_ref[...] = jnp.zeros_like(state_ref)

    c = GDN_CHUNK
    n_chunks = q_ref.shape[1] // c
    row = lax.broadcasted_iota(I32, (c, c), 0)
    col = lax.broadcasted_iota(I32, (c, c), 1)
    incl = row >= col
    strict = row > col
    eye = jnp.where(row == col, 1.0, 0.0)
    pairs = [(ci, h) for ci in range(n_chunks) for h in range(GDN_HEADS)]

    def rows(ci):
        return slice(ci * c, (ci + 1) * c)

    def cols(h):
        return slice(h * HEAD_DIM, (h + 1) * HEAD_DIM)

    q, k, kb, decay, eg, gc, rhs = {}, {}, {}, {}, {}, {}, {}
    for ci, h in pairs:
        g = (ci, h)
        q[g] = q_ref[0, rows(ci), cols(h)]
        k[g] = k_ref[0, rows(ci), cols(h)]
        beta = val_ref[0, rows(ci), S_GB + h:S_GB + h + 1]
        gc[g] = cs_ref[0, rows(ci), S_GA + h:S_GA + h + 1]
        gr = cst_ref[0, S_GA + h:S_GA + h + 1, rows(ci)]
        decay[g] = jnp.where(incl, jnp.exp(jnp.where(incl, gc[g] - gr, 0.0)), 0.0)
        eg[g] = jnp.exp(gc[g])
        kb[g] = k[g] * beta
        rhs[g] = jnp.concatenate([v_ref[0, rows(ci), cols(h)] * beta, kb[g] * eg[g]], axis=1)
    a = {g: jnp.where(strict, _dot_nt(kb[g], k[g]) * decay[g], 0.0) for g in pairs}
    qk = {g: jnp.where(incl, _dot_nt(q[g], k[g]) * decay[g], 0.0) for g in pairs}
    off = (row // 2 == col // 2) & (row != col)
    tinv = {g: eye - jnp.where(off, a[g], 0.0) for g in pairs}
    sz = 2
    while sz < c:
        off = (row // (2 * sz) == col // (2 * sz)) & (row // sz != col // sz)
        left = {g: _dot(tinv[g], jnp.where(off, a[g], 0.0)) for g in pairs}
        tinv = {g: tinv[g] - _dot(left[g], tinv[g]) for g in pairs}
        sz *= 2
    sol = {g: _dot(tinv[g], rhs[g]) for g in pairs}
    state = [state_ref[h] for h in range(GDN_HEADS)]
    heads = range(GDN_HEADS)
    for ci in range(n_chunks):
        ws = [_dot(sol[ci, h][:, HEAD_DIM:], state[h]) for h in heads]
        qs = [_dot(q[ci, h] * eg[ci, h], state[h]) for h in heads]
        v_new = [sol[ci, h][:, :HEAD_DIM] - ws[h] for h in heads]
        intra = [_dot(qk[ci, h], v_new[h]) for h in heads]
        for h in heads:
            g_last = gc[ci, h][c - 1:c, :]
            k_dec = k[ci, h] * jnp.exp(g_last - gc[ci, h])
            state[h] = state[h] * jnp.exp(g_last) + _dot_tn(k_dec, v_new[h])
        for h in heads:
            o = _rms(qs[h] + intra[h], gn_ref[...]) * _silu(z_ref[0, rows(ci), cols(h)])
            o_ref[0, rows(ci), cols(h)] = o.astype(o_ref.dtype)
    for h in heads:
        state_ref[h] = state[h]


def _gdn(cv, p3, val, cs, cst, gain):
    b, s, _ = cv.shape
    t = _tile(s, GDN_ROWS)
    wide = lambda blk: pl.BlockSpec((1, t, GDN_W), lambda bi, i: (bi, i, blk))
    small = pl.BlockSpec((1, t, LANES), lambda bi, i: (bi, i, 0))
    return pl.pallas_call(
        _gdn_kernel,
        out_shape=jax.ShapeDtypeStruct((b, s, GDN_W), BF16),
        grid=(b, s // t),
        in_specs=[wide(C_GQ // GDN_W), wide(C_GK // GDN_W), wide(C_GV // GDN_W),
                  wide(P_GZ // GDN_W), small, small,
                  pl.BlockSpec((1, LANES, t), lambda bi, i: (bi, 0, i)),
                  pl.BlockSpec((1, HEAD_DIM), lambda bi, i: (0, 0))],
        out_specs=pl.BlockSpec((1, t, GDN_W), lambda bi, i: (bi, i, 0)),
        scratch_shapes=[pltpu.VMEM((GDN_HEADS, HEAD_DIM, HEAD_DIM), F32)],
        compiler_params=_params(("parallel", "arbitrary")),
        name="gated_deltanet",
    )(cv, cv, cv, p3, val, cs, cst, gain)


SSM_PAIRS = SSM_HEADS // 2
PAIRS_PER_GROUP = SSM_PAIRS // SSM_GROUPS


def _ssd_kernel(x_ref, b_ref, c_ref, z_ref, val_ref, cs_ref, cst_ref, par_ref, o_ref, st_ref):
    @pl.when(pl.program_id(1) == 0)
    def _():
        st_ref[...] = jnp.zeros_like(st_ref)

    n = SSM_CHUNK
    row = lax.broadcasted_iota(I32, (n, n), 0)
    col = lax.broadcasted_iota(I32, (n, n), 1)
    incl = row >= col
    lo = lax.broadcasted_iota(I32, (n, LANES), 1) < SSM_HEAD_DIM
    lo_row = lo[0:1, :]
    groups = range(SSM_GROUPS)
    pairs = range(SSM_PAIRS)
    lanes_of = lambda p: slice(p * LANES, (p + 1) * LANES)
    grp = lambda p: p // PAIRS_PER_GROUP
    st = [st_ref[p] for p in pairs]

    for ci in range(x_ref.shape[1] // n):
        rs = slice(ci * n, (ci + 1) * n)

        def decay_of(h):
            ac = cs_ref[0, rs, S_DT + h:S_DT + h + 1]
            ar = cst_ref[0, S_DT + h:S_DT + h + 1, rs]
            return jnp.where(incl, jnp.exp(jnp.where(incl, ac - ar, 0.0)), 0.0), ac

        bg = [b_ref[0, rs, g * SSM_STATE:(g + 1) * SSM_STATE] for g in groups]
        cg = [c_ref[0, rs, g * SSM_STATE:(g + 1) * SSM_STATE] for g in groups]
        cb = [_dot_nt(cg[g], bg[g]) for g in groups]
        bg_t = [bg[g].T for g in groups]
        xp, xdt, dec, e_ac, tail, st_dec = [], [], [], [], [], []
        for p in pairs:
            h0, h1 = 2 * p, 2 * p + 1
            xp.append(x_ref[0, rs, lanes_of(p)])
            dt = jnp.where(lo, val_ref[0, rs, S_DT + h0:S_DT + h0 + 1],
                           val_ref[0, rs, S_DT + h1:S_DT + h1 + 1])
            xdt.append(xp[p] * dt)
            d0, ac0 = decay_of(h0)
            d1, ac1 = decay_of(h1)
            dec.append((d0, d1))
            e_ac.append(jnp.where(lo, jnp.exp(ac0), jnp.exp(ac1)))
            al0 = ac0[n - 1:n, :]
            al1 = ac1[n - 1:n, :]
            tail.append(jnp.where(lo, jnp.exp(al0 - ac0), jnp.exp(al1 - ac1)) * xdt[p])
            st_dec.append(jnp.where(lo_row, jnp.exp(al0), jnp.exp(al1)))
        diag0 = [_dot(cb[grp(p)] * dec[p][0], xdt[p]) for p in pairs]
        diag1 = [_dot(cb[grp(p)] * dec[p][1], xdt[p]) for p in pairs]
        y_off = [_dot(cg[grp(p)], st[p]) for p in pairs]
        st_add = [_dot(bg_t[grp(p)], tail[p]) for p in pairs]
        ys = []
        for p in pairs:
            st[p] = st[p] * st_dec[p] + st_add[p]
            y_diag = jnp.where(lo, diag0[p], diag1[p])
            ys.append((y_diag + y_off[p] * e_ac[p] + par_ref[0:1, lanes_of(p)] * xp[p])
                      * _silu(z_ref[0, rs, lanes_of(p)]))
        for g in groups:
            mine = [p for p in pairs if grp(p) == g]
            ssq = sum(jnp.sum(ys[p] * ys[p], axis=-1, keepdims=True) for p in mine)
            inv = lax.rsqrt(ssq / (PAIRS_PER_GROUP * LANES) + NORM_EPS)
            for p in mine:
                o_ref[0, rs, lanes_of(p)] = (ys[p] * inv * par_ref[1:2, lanes_of(p)]).astype(o_ref.dtype)
    for p in pairs:
        st_ref[p] = st[p]


def _ssd(cv, p3, val, cs, cst, par):
    b, s, _ = cv.shape
    t = _tile(s, SSD_ROWS)
    small = pl.BlockSpec((1, t, LANES), lambda bi, i: (bi, i, 0))
    return pl.pallas_call(
        _ssd_kernel,
        out_shape=jax.ShapeDtypeStruct((b, s, SSM_W), BF16),
        grid=(b, s // t),
        in_specs=[pl.BlockSpec((1, t, SSM_W), lambda bi, i: (bi, i, C_SX // SSM_W)),
                  pl.BlockSpec((1, t, BC_W), lambda bi, i: (bi, i, C_SB // BC_W)),
                  pl.BlockSpec((1, t, BC_W), lambda bi, i: (bi, i, C_SC // BC_W)),
                  pl.BlockSpec((1, t, SSM_W), lambda bi, i: (bi, i, P_SZ // SSM_W)),
                  small, small,
                  pl.BlockSpec((1, LANES, t), lambda bi, i: (bi, 0, i)),
                  pl.BlockSpec((8, SSM_W), lambda bi, i: (0, 0))],
        out_specs=pl.BlockSpec((1, t, SSM_W), lambda bi, i: (bi, i, 0)),
        scratch_shapes=[pltpu.VMEM((SSM_PAIRS, SSM_STATE, LANES), F32)],
        compiler_params=_params(("parallel", "arbitrary")),
        name="ssd_scan",
    )(cv, cv, cv, p3, val, cs, cst, par)


def _outproj_kernel(a1_ref, a2_ref, a3_ref, w_ref, x_ref, o_ref):
    r1 = a1_ref.shape[1]
    r2 = r1 + a2_ref.shape[1]
    o_ref[...] = (x_ref[...] + _dot(a1_ref[...], w_ref[0, :r1, :]) + _dot(a2_ref[...], w_ref[0, r1:r2, :])
                  + _dot(a3_ref[...], w_ref[0, r2:, :]))


def _outproj(a1, a2, a3, w, layer, x):
    n, d = x.shape
    tm = _tile(n, 1024)
    tn = _tile(d, 1024)
    act = lambda a: pl.BlockSpec((tm, a.shape[1]), lambda i, j: (i, 0))
    return pl.pallas_call(
        _outproj_kernel,
        out_shape=jax.ShapeDtypeStruct((n, d), F32),
        grid=(n // tm, d // tn),
        in_specs=[act(a1), act(a2), act(a3),
                  pl.BlockSpec((1, w.shape[1], tn), lambda i, j: (layer, 0, j)),
                  pl.BlockSpec((tm, tn), lambda i, j: (i, j))],
        out_specs=pl.BlockSpec((tm, tn), lambda i, j: (i, j)),
        compiler_params=_params(("parallel", "parallel")),
        name="out_proj",
    )(a1, a2, a3, w, x)


def _ffn_kernel(x_ref, g_ref, wg_ref, wu_ref, wd_ref, o_ref, h_ref):
    @pl.when(pl.program_id(1) == 0)
    def _():
        x = x_ref[...]
        h_ref[...] = _rms(x, g_ref[...]).astype(BF16)
        o_ref[...] = x

    h = h_ref[...]
    act = _silu(_dot(h, wg_ref[0])) * _dot(h, wu_ref[0])
    o_ref[...] += _dot(act.astype(BF16), wd_ref[0])


def _ffn(x, gain, wg, wu, wd, layer):
    n, d = x.shape
    f = wg.shape[2]
    tm = _tile(n, 1024)
    tf = _tile(f, 512)
    return pl.pallas_call(
        _ffn_kernel,
        out_shape=jax.ShapeDtypeStruct((n, d), F32),
        grid=(n // tm, f // tf),
        in_specs=[pl.BlockSpec((tm, d), lambda i, j: (i, 0)),
                  pl.BlockSpec((1, d), lambda i, j: (0, 0)),
                  pl.BlockSpec((1, d, tf), lambda i, j: (layer, 0, j)),
                  pl.BlockSpec((1, d, tf), lambda i, j: (layer, 0, j)),
                  pl.BlockSpec((1, tf, d), lambda i, j: (layer, j, 0))],
        out_specs=pl.BlockSpec((tm, d), lambda i, j: (i, 0)),
        scratch_shapes=[pltpu.VMEM((tm, d), BF16)],
        compiler_params=_params(("parallel", "arbitrary")),
        name="dense_ffn",
    )(x, gain, wg, wu, wd)


def _router_kernel(x_ref, g_ref, r_ref, idx_ref, w_ref):
    h = _rms(x_ref[...], g_ref[...])
    r = r_ref[...]
    h_hi = h.astype(BF16)
    r_hi = r.astype(BF16)
    h_lo = (h - h_hi.astype(F32)).astype(BF16)
    r_lo = (r - r_hi.astype(F32)).astype(BF16)
    logits = _dot(h_hi, r_hi) + (_dot(h_hi, r_lo) + _dot(h_lo, r_hi))
    lane = lax.broadcasted_iota(I32, logits.shape, 1)
    l1 = jnp.where(lane < N_EXPERTS, logits, NEG_BIG)
    m1 = jnp.max(l1, axis=-1, keepdims=True)
    i1 = jnp.min(jnp.where(l1 == m1, lane, LANES), axis=-1, keepdims=True)
    l2 = jnp.where(lane == i1, NEG_BIG, l1)
    m2 = jnp.max(l2, axis=-1, keepdims=True)
    i2 = jnp.min(jnp.where(l2 == m2, lane, LANES), axis=-1, keepdims=True)
    e = jnp.exp(m2 - m1)
    idx_ref[...] = jnp.where(lane == 0, i1, jnp.where(lane == 1, i2, 0))
    w_ref[...] = jnp.where(lane == 0, 1.0 / (1.0 + e), jnp.where(lane == 1, e / (1.0 + e), 0.0))


def _router(x, gain, router_pad):
    n, d = x.shape
    tm = _tile(n, 512)
    return pl.pallas_call(
        _router_kernel,
        out_shape=(jax.ShapeDtypeStruct((n, LANES), I32), jax.ShapeDtypeStruct((n, LANES), F32)),
        grid=(n // tm,),
        in_specs=[pl.BlockSpec((tm, d), lambda i: (i, 0)),
                  pl.BlockSpec((1, d), lambda i: (0, 0)),
                  pl.BlockSpec((d, LANES), lambda i: (0, 0))],
        out_specs=(pl.BlockSpec((tm, LANES), lambda i: (i, 0)),
                   pl.BlockSpec((tm, LANES), lambda i: (i, 0))),
        compiler_params=_params(("parallel",)),
        name="moe_router",
    )(x, gain, router_pad)


def _row_copy(src_hbm, src_row, dst, dst_row, sem):
    return pltpu.make_async_copy(src_hbm.at[pl.ds(src_row, 1)], dst.at[pl.ds(dst_row, 1)], sem)


DMA_UNROLL = 8


def _dispatch_kernel(d0_ref, d1_ref, pad_ref, x_ref, g_ref, xs_hbm, hbuf, sems):
    i = pl.program_id(0)
    last = pl.num_programs(0) - 1
    rows = x_ref.shape[0]
    n_fill = pad_ref.shape[2]
    slot = lax.rem(i, 2)

    def wait_block(s):
        def body(r, carry):
            _row_copy(hbuf.at[s], r, xs_hbm, 0, sems.at[s]).wait()
            _row_copy(hbuf.at[s], r, xs_hbm, 0, sems.at[s]).wait()
            return carry

        def fill_body(r, carry):
            _row_copy(hbuf.at[s], 0, xs_hbm, 0, sems.at[s]).wait()
            return carry
        lax.fori_loop(0, rows, body, 0, unroll=DMA_UNROLL)
        lax.fori_loop(0, n_fill, fill_body, 0, unroll=DMA_UNROLL)

    @pl.when(i >= 2)
    def _():
        wait_block(slot)

    hbuf[slot] = _rms(x_ref[...], g_ref[...])

    def start(r, carry):
        _row_copy(hbuf.at[slot], r, xs_hbm, d0_ref[0, 0, r], sems.at[slot]).start()
        _row_copy(hbuf.at[slot], r, xs_hbm, d1_ref[0, 0, r], sems.at[slot]).start()
        return carry

    def start_fill(r, carry):
        _row_copy(hbuf.at[slot], 0, xs_hbm, pad_ref[0, 0, r], sems.at[slot]).start()
        return carry

    lax.fori_loop(0, rows, start, 0, unroll=DMA_UNROLL)
    lax.fori_loop(0, n_fill, start_fill, 0, unroll=DMA_UNROLL)

    @pl.when(i == last)
    def _():
        wait_block(slot)

    @pl.when((i == last) & (i >= 1))
    def _():
        wait_block(1 - slot)


def _dispatch(x, gain, dest0, dest1, pad_slots, n_slots):
    n, d = x.shape
    tb = _tile(n, 256)
    nb = n // tb
    n_fill = pad_slots.shape[0] // nb
    assert n_fill * nb == pad_slots.shape[0]
    dspec = pl.BlockSpec((1, 1, tb), lambda i: (i, 0, 0), memory_space=pltpu.SMEM)
    return pl.pallas_call(
        _dispatch_kernel,
        out_shape=jax.ShapeDtypeStruct((n_slots, d), F32),
        grid=(nb,),
        in_specs=[dspec, dspec,
                  pl.BlockSpec((1, 1, n_fill), lambda i: (i, 0, 0), memory_space=pltpu.SMEM),
                  pl.BlockSpec((tb, d), lambda i: (i, 0)),
                  pl.BlockSpec((1, d), lambda i: (0, 0))],
        out_specs=pl.BlockSpec(memory_space=pl.ANY),
        scratch_shapes=[pltpu.VMEM((2, tb, d), F32), pltpu.SemaphoreType.DMA((2,))],
        compiler_params=_params(("arbitrary",)),
        name="moe_dispatch",
    )(dest0.reshape(nb, 1, tb), dest1.reshape(nb, 1, tb), pad_slots.reshape(nb, 1, n_fill), x, gain)


def _expert_ffn_kernel(e_ref, na_ref, x_ref, wg_ref, wu_ref, wd_ref, o_ref, xb_ref):
    s = pl.program_id(0)
    f = pl.program_id(1)

    @pl.when(f == 0)
    def _():
        o_ref[...] = jnp.zeros_like(o_ref)
        xb_ref[...] = x_ref[...].astype(BF16)

    @pl.when(s < na_ref[0])
    def _():
        x = xb_ref[...]
        gate = _dot(x, wg_ref[0, 0].astype(BF16))
        up = _dot(x, wu_ref[0, 0].astype(BF16))
        o_ref[...] += _dot((_silu(gate) * up).astype(BF16), wd_ref[0, 0].astype(BF16))


def _expert_ffn(xs, blk_expert, n_active, wg, wu, wd, layer):
    n_slots, d = xs.shape
    f = wg.shape[3]
    nblk = n_slots // MOE_ROWS
    tf = _tile(f, 256)
    nf = f // tf

    def fidx(s, j, na_ref):
        return jnp.where(s < na_ref[0], j, nf - 1)

    grid_spec = pltpu.PrefetchScalarGridSpec(
        num_scalar_prefetch=2,
        grid=(nblk, nf),
        in_specs=[pl.BlockSpec((MOE_ROWS, d), lambda s, j, e, na: (s, 0)),
                  pl.BlockSpec((1, 1, d, tf), lambda s, j, e, na: (layer, e[s], 0, fidx(s, j, na))),
                  pl.BlockSpec((1, 1, d, tf), lambda s, j, e, na: (layer, e[s], 0, fidx(s, j, na))),
                  pl.BlockSpec((1, 1, tf, d), lambda s, j, e, na: (layer, e[s], fidx(s, j, na), 0))],
        out_specs=pl.BlockSpec((MOE_ROWS, d), lambda s, j, e, na: (s, 0)),
        scratch_shapes=[pltpu.VMEM((MOE_ROWS, d), BF16)],
    )
    return pl.pallas_call(
        _expert_ffn_kernel,
        out_shape=jax.ShapeDtypeStruct((n_slots, d), F32),
        grid_spec=grid_spec,
        compiler_params=_params(("arbitrary", "arbitrary")),
        name="moe_expert_ffn",
    )(blk_expert, n_active, xs, wg, wu, wd)


def _combine_kernel(d0_ref, d1_ref, d0_next_ref, d1_next_ref, x_ref, w_ref, g_ref, y_hbm, o_ref,
                    buf, sems, *, final_norm):
    i = pl.program_id(0)
    rows = o_ref.shape[0]
    slot = lax.rem(i, 2)

    def start_block(a_ref, b_ref, s):
        def body(r, carry):
            _row_copy(y_hbm, a_ref[0, 0, r], buf.at[s, 0], r, sems.at[s]).start()
            _row_copy(y_hbm, b_ref[0, 0, r], buf.at[s, 1], r, sems.at[s]).start()
            return carry
        lax.fori_loop(0, rows, body, 0, unroll=DMA_UNROLL)

    @pl.when(i == 0)
    def _():
        start_block(d0_ref, d1_ref, 0)

    @pl.when(i + 1 < pl.num_programs(0))
    def _():
        start_block(d0_next_ref, d1_next_ref, 1 - slot)

    def wait(r, carry):
        _row_copy(y_hbm, 0, buf.at[slot, 0], r, sems.at[slot]).wait()
        _row_copy(y_hbm, 0, buf.at[slot, 1], r, sems.at[slot]).wait()
        return carry

    lax.fori_loop(0, rows, wait, 0, unroll=DMA_UNROLL)
    out = x_ref[...] + w_ref[:, 0:1] * buf[slot, 0] + w_ref[:, 1:2] * buf[slot, 1]
    o_ref[...] = _rms(out, g_ref[...]) if final_norm else out


def _combine(x, top_w, dest0, dest1, ys, final_gain=None):
    n, d = x.shape
    tb = _tile(n, 256)
    nb = n // tb
    dspec = pl.BlockSpec((1, 1, tb), lambda i: (i, 0, 0), memory_space=pltpu.SMEM)
    dnext = pl.BlockSpec((1, 1, tb), lambda i: (jnp.minimum(i + 1, nb - 1), 0, 0),
                         memory_space=pltpu.SMEM)
    d0 = dest0.reshape(nb, 1, tb)
    d1 = dest1.reshape(nb, 1, tb)
    gain = jnp.ones((1, d), F32) if final_gain is None else final_gain
    return pl.pallas_call(
        functools.partial(_combine_kernel, final_norm=final_gain is not None),
        out_shape=jax.ShapeDtypeStruct((n, d), F32),
        grid=(nb,),
        in_specs=[dspec, dspec, dnext, dnext,
                  pl.BlockSpec((tb, d), lambda i: (i, 0)),
                  pl.BlockSpec((tb, LANES), lambda i: (i, 0)),
                  pl.BlockSpec((1, d), lambda i: (0, 0)),
                  pl.BlockSpec(memory_space=pl.ANY)],
        out_specs=pl.BlockSpec((tb, d), lambda i: (i, 0)),
        scratch_shapes=[pltpu.VMEM((2, 2, tb, d), F32), pltpu.SemaphoreType.DMA((2,))],
        compiler_params=_params(("arbitrary",)),
        name="moe_combine",
    )(d0, d1, d0, d1, x, top_w, gain, ys)


def _moe(x, gain, router, wg, wu, wd, layer, final_gain=None):
    n, d = x.shape
    router_pad = jnp.pad(router, ((0, 0), (0, LANES - N_EXPERTS)))
    top_idx, top_w = _router(x, gain, router_pad)
    flat_e = top_idx[:, :TOP_K].reshape(-1)
    onehot = (flat_e[:, None] == jnp.arange(N_EXPERTS, dtype=I32)[None, :]).astype(I32)
    rank = jnp.take_along_axis(jnp.cumsum(onehot, axis=0) - onehot, flat_e[:, None], axis=1)[:, 0]
    counts = jnp.sum(onehot, axis=0)
    padded = (counts + MOE_ROWS - 1) // MOE_ROWS * MOE_ROWS
    pad_end = jnp.cumsum(padded)
    dest = (pad_end - padded)[flat_e] + rank
    nblk = (n * TOP_K) // MOE_ROWS + N_EXPERTS
    n_slots = nblk * MOE_ROWS
    n_active = (pad_end[-1] // MOE_ROWS).astype(I32)
    blk_start = jnp.arange(nblk, dtype=I32) * MOE_ROWS
    blk_expert = jnp.minimum(jnp.searchsorted(pad_end, blk_start, side='right'), N_EXPERTS - 1)
    last_e = blk_expert[jnp.maximum(n_active - 1, 0)]
    blk_expert = jnp.where(jnp.arange(nblk) < n_active, blk_expert, last_e).astype(I32)

    pad_sizes = jnp.concatenate([padded - counts, (n_slots - pad_end[-1])[None]])
    pad_first = jnp.concatenate([pad_end - padded + counts, pad_end[-1:]])
    pad_cum = jnp.cumsum(pad_sizes)
    k = jnp.arange(n_slots - n * TOP_K, dtype=I32)
    seg = jnp.searchsorted(pad_cum, k, side='right')
    pad_slots = (pad_first[seg] + k - (pad_cum - pad_sizes)[seg]).astype(I32)

    dest2 = dest.reshape(n, TOP_K)
    xs = _dispatch(x, gain, dest2[:, 0], dest2[:, 1], pad_slots, n_slots)
    ys = _expert_ffn(xs, blk_expert, n_active.reshape(1), wg, wu, wd, layer)
    return _combine(x, top_w, dest2[:, 0], dest2[:, 1], ys, final_gain)


def _final_norm_kernel(x_ref, g_ref, o_ref):
    o_ref[...] = _rms(x_ref[...], g_ref[...])


def _final_norm(x, gain):
    n, d = x.shape
    tm = _tile(n, 512)
    return pl.pallas_call(
        _final_norm_kernel,
        out_shape=jax.ShapeDtypeStruct((n, d), F32),
        grid=(n // tm,),
        in_specs=[pl.BlockSpec((tm, d), lambda i: (i, 0)), pl.BlockSpec((1, d), lambda i: (0, 0))],
        out_specs=pl.BlockSpec((tm, d), lambda i: (i, 0)),
        compiler_params=_params(("parallel",)),
        name="final_norm",
    )(x, gain)


def _layout_w_in(w_in):
    offs = [0]
    for sz in PROJ_SIZES:
        offs.append(offs[-1] + sz)
    seg = [w_in[..., offs[k]:offs[k + 1]] for k in range(len(PROJ_SIZES))]
    fq, fk, fv, ff, gqkv, gz, gb, ga, sz_, sxbc, sdt = seg
    lead = w_in.shape[:-1]
    small = jnp.concatenate([ff, gb, ga, sdt, jnp.zeros(lead + (LANES - S_END,), w_in.dtype)], -1)
    tail = jnp.zeros(lead + (P_WIDTH - P_SMALL - LANES,), w_in.dtype)
    return jnp.concatenate([fq, fk, fv, gqkv, gz, sz_, sxbc, small, tail], -1).astype(BF16)


def _row_tile(rows, width):
    out = jnp.zeros((8, width), F32)
    for r, v in enumerate(rows):
        out = out.at[r, :v.shape[0]].set(v.astype(F32))
    return out


def kernel(x, norm_mix, w_in, fox_f_bias, fox_out_norm, gdn_conv_w, gdn_A_log, gdn_dt_bias,
           gdn_out_norm, ssm_conv_w, ssm_conv_b, ssm_A_log, ssm_dt_bias, ssm_D, ssm_out_norm,
           w_out, norm_ffn, ffn_w_gate, ffn_w_up, ffn_w_down, moe_router, moe_w_gate,
           moe_w_up, moe_w_down, norm_final):
    b, s, d = x.shape
    depth = w_in.shape[0]
    n = b * s
    w_in_l = _layout_w_in(w_in)
    w_out_b = w_out.astype(BF16)
    ffn_g, ffn_u, ffn_d = (w.astype(BF16) for w in (ffn_w_gate, ffn_w_up, ffn_w_down))
    conv_w = jnp.concatenate([gdn_conv_w, ssm_conv_w], axis=-1)
    conv_b = jnp.concatenate([jnp.zeros((depth, 3 * GDN_W), F32), ssm_conv_b], axis=-1)

    xf = x.reshape(n, d)
    for layer in range(depth):
        p = _norm_matmul(xf, norm_mix[layer][None, :], w_in_l, layer)
        p3 = p.reshape(b, s, P_WIDTH)
        zeros = lambda k: jnp.zeros((k,), F32)
        bias_row = jnp.concatenate([fox_f_bias[layer], zeros(GDN_HEADS), gdn_dt_bias[layer],
                                    ssm_dt_bias[layer]])
        alog_row = jnp.concatenate([zeros(S_GA), gdn_A_log[layer], ssm_A_log[layer]])
        val, cs, cst = _prep(p3, _row_tile([bias_row, alog_row], LANES))
        cv = _conv(p3, conv_w[layer], conv_b[layer][None, :])
        o_fox = _fox(p3, cst, fox_out_norm[layer].reshape(1, FOX_W))
        o_gdn = _gdn(cv, p3, val, cs, cst, gdn_out_norm[layer][None, :])
        ssm_par = _row_tile([jnp.repeat(ssm_D[layer], SSM_HEAD_DIM), ssm_out_norm[layer]], SSM_W)
        o_ssm = _ssd(cv, p3, val, cs, cst, ssm_par)
        xf = _outproj(o_fox.reshape(n, FOX_W), o_gdn.reshape(n, GDN_W), o_ssm.reshape(n, SSM_W),
                      w_out_b, layer, xf)
        j = layer // 2
        gain = norm_ffn[layer][None, :]
        if layer % 2 == 0:
            xf = _ffn(xf, gain, ffn_g, ffn_u, ffn_d, j)
        else:
            fused_final = norm_final[None, :] if layer == depth - 1 else None
            xf = _moe(xf, gain, moe_router[j], moe_w_gate, moe_w_up, moe_w_down, j, fused_final)
    if depth % 2:
        xf = _final_norm(xf, norm_final[None, :])
    return xf.reshape(b, s, d)
```

```python
import functools

import jax
import jax.numpy as jnp
from jax import lax
from jax.experimental import pallas as pl
from jax.experimental.pallas import tpu as pltpu

F32 = jnp.float32
BF16 = jnp.bfloat16
I32 = jnp.int32

NORM_EPS = 1e-6
LANES = 128
NEG_BIG = -1e30

FOX_HEADS = 4
HEAD_DIM = 128
GDN_HEADS = 6
GDN_CHUNK = 64
GDN_ROWS = 256
SSM_HEADS = 12
SSM_HEAD_DIM = 64
SSM_STATE = 128
SSM_GROUPS = 2
SSM_CHUNK = 128
SSD_ROWS = 256
PREP_ROWS = 512
CONV_WIDTH = 4
N_EXPERTS = 8
TOP_K = 2

FOX_W = FOX_HEADS * HEAD_DIM
GDN_W = GDN_HEADS * HEAD_DIM
SSM_W = SSM_HEADS * SSM_HEAD_DIM
BC_W = SSM_GROUPS * SSM_STATE
PROJ_SIZES = (FOX_W, FOX_W, FOX_W, FOX_HEADS, 3 * GDN_W, GDN_W, GDN_HEADS, GDN_HEADS,
              SSM_W, SSM_W + 2 * BC_W, SSM_HEADS)

P_FQ, P_FK, P_FV = 0, FOX_W, 2 * FOX_W
P_GQKV = 3 * FOX_W
P_GZ = P_GQKV + 3 * GDN_W
P_SZ = P_GZ + GDN_W
P_SXBC = P_SZ + SSM_W
P_SMALL = P_SXBC + SSM_W + 2 * BC_W
P_WIDTH = 6912
S_FF, S_GB, S_GA, S_DT = 0, 4, 10, 16
S_END = S_DT + SSM_HEADS
C_GQ, C_GK, C_GV = 0, GDN_W, 2 * GDN_W
C_SX = 3 * GDN_W
C_SB = C_SX + SSM_W
C_SC = C_SB + BC_W
C_WIDTH = C_SC + BC_W

MOE_ROWS = 1024
VMEM_LIMIT = 56 * 1024 * 1024


def _tile(n, pref):
    t = min(n, pref)
    while n % t:
        t //= 2
    return t


def _params(sem, vmem=VMEM_LIMIT):
    return pltpu.CompilerParams(dimension_semantics=sem, vmem_limit_bytes=vmem)


def _silu(x):
    return x / (1.0 + jnp.exp(-x))


def _softplus(x):
    return jnp.maximum(x, 0.0) + jnp.log1p(jnp.exp(-jnp.abs(x)))


def _rms(x, gain):
    return x * lax.rsqrt(jnp.mean(x * x, axis=-1, keepdims=True) + NORM_EPS) * gain


def _dot(a, b):
    return jnp.dot(a, b, preferred_element_type=F32)


def _dot_nt(a, b):
    return lax.dot_general(a, b, (((1,), (1,)), ((), ())), preferred_element_type=F32)


def _dot_tn(a, b):
    return lax.dot_general(a, b, (((0,), (0,)), ((), ())), preferred_element_type=F32)


def _norm_matmul_kernel(x_ref, g_ref, w_ref, o_ref, h_ref):
    @pl.when(pl.program_id(1) == 0)
    def _():
        h_ref[...] = _rms(x_ref[...], g_ref[...]).astype(BF16)

    o_ref[...] = _dot(h_ref[...], w_ref[0])


def _norm_matmul(x, gain, w, layer):
    n, d = x.shape
    nout = w.shape[2]
    tm = _tile(n, 1024)
    tn = _tile(nout, 768)
    return pl.pallas_call(
        _norm_matmul_kernel,
        out_shape=jax.ShapeDtypeStruct((n, nout), F32),
        grid=(n // tm, nout // tn),
        in_specs=[pl.BlockSpec((tm, d), lambda i, j: (i, 0)),
                  pl.BlockSpec((1, d), lambda i, j: (0, 0)),
                  pl.BlockSpec((1, d, tn), lambda i, j: (layer, 0, j))],
        out_specs=pl.BlockSpec((tm, tn), lambda i, j: (i, j)),
        scratch_shapes=[pltpu.VMEM((tm, d), BF16)],
        compiler_params=_params(("parallel", "arbitrary")),
        name="norm_inproj",
    )(x, gain, w)


def _prep_kernel(p_ref, par_ref, val_ref, cs_ref, cst_ref, carry_ref):
    @pl.when(pl.program_id(1) == 0)
    def _():
        carry_ref[...] = jnp.zeros_like(carry_ref)

    blk = SSM_CHUNK
    lane = lax.broadcasted_iota(I32, (blk, LANES), 1)
    row = lax.broadcasted_iota(I32, (blk, blk), 0)
    col = lax.broadcasted_iota(I32, (blk, blk), 1)
    tri = row >= col
    tri_blk = jnp.where(tri, 1.0, 0.0)
    tri_gdn = jnp.where(tri & (row // GDN_CHUNK == col // GDN_CHUNK), 1.0, 0.0)
    neg_a = -jnp.exp(par_ref[1:2, :])
    carry = carry_ref[...]
    for sb in range(p_ref.shape[1] // blk):
        rs = slice(sb * blk, (sb + 1) * blk)
        v = p_ref[0, rs, :] + par_ref[0:1, :]
        sp = _softplus(v)
        log_f = -_softplus(-v)
        beta = 1.0 / (1.0 + jnp.exp(-v))
        val_ref[0, rs, :] = jnp.where(lane < S_GA, beta, sp)
        z = jnp.where(lane < S_GB, log_f, jnp.where(lane < S_GA, 0.0, neg_a * sp))
        z = jnp.where(lane < S_END, z, 0.0)
        cs_blk = jnp.dot(tri_blk, z, precision=lax.Precision.HIGHEST, preferred_element_type=F32)
        cs_gdn = jnp.dot(tri_gdn, z, precision=lax.Precision.HIGHEST, preferred_element_type=F32)
        cs_run = cs_blk + carry
        carry = cs_run[blk - 1:blk, :]
        cs = jnp.where(lane < S_GB, cs_run, jnp.where(lane < S_DT, cs_gdn, cs_blk))
        cs_ref[0, rs, :] = cs
        cst_ref[0, :, rs] = cs.T
    carry_ref[...] = carry


def _prep(p3, par):
    b, s, _ = p3.shape
    blk = _tile(s, PREP_ROWS)
    shp = jax.ShapeDtypeStruct((b, s, LANES), F32)
    return pl.pallas_call(
        _prep_kernel,
        out_shape=(shp, shp, jax.ShapeDtypeStruct((b, LANES, s), F32)),
        grid=(b, s // blk),
        in_specs=[pl.BlockSpec((1, blk, LANES), lambda bi, i: (bi, i, P_SMALL // LANES)),
                  pl.BlockSpec((8, LANES), lambda bi, i: (0, 0))],
        out_specs=(pl.BlockSpec((1, blk, LANES), lambda bi, i: (bi, i, 0)),
                   pl.BlockSpec((1, blk, LANES), lambda bi, i: (bi, i, 0)),
                   pl.BlockSpec((1, LANES, blk), lambda bi, i: (bi, 0, i))),
        scratch_shapes=[pltpu.VMEM((1, LANES), F32)],
        compiler_params=_params(("parallel", "arbitrary")),
        name="gate_prep",
    )(p3, par)


CONV_COLS = 256
CONV_GDN_BLOCKS = 3 * GDN_W // CONV_COLS
CONV_L2_BLOCKS = 2 * GDN_W // CONV_COLS
CONV_Q_BLOCKS = GDN_W // CONV_COLS


CONV_STRIP = 64


def _conv_kernel(u_ref, halo_ref, w_ref, b_ref, o_ref, edge_ref):
    i = pl.program_id(1)
    c = pl.program_id(2)
    t = u_ref.shape[1]
    strip = min(CONV_STRIP, t)
    edge_ref[0:8, :] = jnp.where(i > 0, halo_ref[0], 0.0)
    edge_ref[8:, :] = u_ref[0, 0:strip, :]
    taps = [w_ref[k:k + 1, :] for k in range(CONV_WIDTH)]
    bias = b_ref[...]

    def conv_strip(r0):
        window = edge_ref[...] if r0 == 0 else u_ref[0, r0 - 8:r0 + strip, :]
        acc = bias + window[8:] * taps[CONV_WIDTH - 1]
        for back in range(1, CONV_WIDTH):
            acc = acc + pltpu.roll(window, back, 0)[8:] * taps[CONV_WIDTH - 1 - back]
        return _silu(acc)

    @pl.when(c < CONV_L2_BLOCKS)
    def _():
        scale = jnp.where(c < CONV_Q_BLOCKS, HEAD_DIM ** -0.5, 1.0)
        for r0 in range(0, t, strip):
            y = conv_strip(r0)
            for k in range(CONV_COLS // HEAD_DIM):
                yk = y[:, k * HEAD_DIM:(k + 1) * HEAD_DIM]
                inv = lax.rsqrt(jnp.sum(yk * yk, axis=-1, keepdims=True) + NORM_EPS) * scale
                o_ref[0, r0:r0 + strip, k * HEAD_DIM:(k + 1) * HEAD_DIM] = yk * inv

    @pl.when(c >= CONV_L2_BLOCKS)
    def _():
        for r0 in range(0, t, strip):
            o_ref[0, r0:r0 + strip, :] = conv_strip(r0)


def _conv(p3, cw, cb):
    b, s, _ = p3.shape
    t = _tile(s, 1024)
    gdn0 = P_GQKV // CONV_COLS
    ssm_shift = P_SXBC // CONV_COLS - CONV_GDN_BLOCKS

    def col(c):
        return jnp.where(c < CONV_GDN_BLOCKS, c + gdn0, c + ssm_shift)

    return pl.pallas_call(
        _conv_kernel,
        out_shape=jax.ShapeDtypeStruct((b, s, C_WIDTH), F32),
        grid=(b, s // t, C_WIDTH // CONV_COLS),
        in_specs=[pl.BlockSpec((1, t, CONV_COLS), lambda bi, i, c: (bi, i, col(c))),
                  pl.BlockSpec((1, 8, CONV_COLS),
                               lambda bi, i, c: (bi, jnp.maximum(i * (t // 8) - 1, 0), col(c))),
                  pl.BlockSpec((CONV_WIDTH, CONV_COLS), lambda bi, i, c: (0, c)),
                  pl.BlockSpec((1, CONV_COLS), lambda bi, i, c: (0, c))],
        out_specs=pl.BlockSpec((1, t, CONV_COLS), lambda bi, i, c: (bi, i, c)),
        scratch_shapes=[pltpu.VMEM((8 + min(CONV_STRIP, t), CONV_COLS), F32)],
        compiler_params=_params(("parallel", "parallel", "parallel")),
        name="conv_silu",
    )(p3, p3, cw, cb)


def _fox_kernel(qi_ref, kj_ref, q_ref, k_ref, v_ref, ck_ref, gn_ref, o_ref, m_ref, acc_ref):
    i = qi_ref[pl.program_id(1)]
    j = kj_ref[pl.program_id(1)]
    tq = q_ref.shape[1]
    tk = k_ref.shape[1]
    cols = lambda h: slice(h * HEAD_DIM, (h + 1) * HEAD_DIM)
    wide = lambda h: slice(2 * h * HEAD_DIM, 2 * (h + 1) * HEAD_DIM)

    @pl.when(j == 0)
    def _():
        m_ref[...] = jnp.full_like(m_ref, NEG_BIG)
        acc_ref[...] = jnp.zeros_like(acc_ref)

    def step(masked):
        if masked:
            causal = (lax.broadcasted_iota(I32, (tq, tk), 1) <= lax.broadcasted_iota(I32, (tq, tk), 0))
        ones = jnp.ones((tk, HEAD_DIM), BF16)
        s, p, alpha = {}, {}, {}

        def logits(h):
            q = (q_ref[0, :, cols(h)] * HEAD_DIM ** -0.5).astype(BF16)
            sh = _dot_nt(q, k_ref[0, :, cols(h)].astype(BF16)) - ck_ref[0, h:h + 1, :]
            s[h] = jnp.where(causal, sh, NEG_BIG) if masked else sh

        def softmax(h):
            m_prev = m_ref[h]
            m_new = jnp.maximum(m_prev, jnp.max(s[h], axis=-1, keepdims=True))
            p[h] = jnp.exp(s[h] - jnp.tile(m_new, (1, tk // LANES))).astype(BF16)
            alpha[h] = jnp.exp(m_prev - m_new)
            m_ref[h] = m_new

        def values(h):
            v1 = jnp.concatenate([v_ref[0, :, cols(h)].astype(BF16), ones], axis=1)
            acc_ref[:, wide(h)] = jnp.tile(alpha[h], (1, 2)) * acc_ref[:, wide(h)] + _dot(p[h], v1)

        for t in range(FOX_HEADS + 2):
            if t < FOX_HEADS:
                logits(t)
            if 0 <= t - 1 < FOX_HEADS:
                softmax(t - 1)
            if 0 <= t - 2 < FOX_HEADS:
                values(t - 2)

    @pl.when(j < i)
    def _():
        step(False)

    @pl.when(j == i)
    def _():
        step(True)
        for h in range(FOX_HEADS):
            both = acc_ref[:, wide(h)]
            o = both[:, :HEAD_DIM] / both[:, HEAD_DIM:]
            o_ref[0, :, cols(h)] = _rms(o, gn_ref[:, cols(h)]).astype(o_ref.dtype)


def _fox(p3, cst, gain):
    b, s, _ = p3.shape
    t = _tile(s, 512)
    n = s // t
    pairs = [(i, j) for i in range(n) for j in range(i + 1)]
    qi = jnp.asarray([p[0] for p in pairs], I32)
    kj = jnp.asarray([p[1] for p in pairs], I32)
    grid_spec = pltpu.PrefetchScalarGridSpec(
        num_scalar_prefetch=2,
        grid=(b, len(pairs)),
        in_specs=[pl.BlockSpec((1, t, FOX_W), lambda bi, p, qi, kj: (bi, qi[p], P_FQ // FOX_W)),
                  pl.BlockSpec((1, t, FOX_W), lambda bi, p, qi, kj: (bi, kj[p], P_FK // FOX_W)),
                  pl.BlockSpec((1, t, FOX_W), lambda bi, p, qi, kj: (bi, kj[p], P_FV // FOX_W)),
                  pl.BlockSpec((1, 8, t), lambda bi, p, qi, kj: (bi, 0, kj[p])),
                  pl.BlockSpec((1, FOX_W), lambda bi, p, qi, kj: (0, 0))],
        out_specs=pl.BlockSpec((1, t, FOX_W), lambda bi, p, qi, kj: (bi, qi[p], 0)),
        scratch_shapes=[pltpu.VMEM((FOX_HEADS, t, LANES), F32),
                        pltpu.VMEM((t, 2 * FOX_W), F32)],
    )
    return pl.pallas_call(
        _fox_kernel,
        out_shape=jax.ShapeDtypeStruct((b, s, FOX_W), BF16),
        grid_spec=grid_spec,
        compiler_params=_params(("parallel", "arbitrary")),
        name="fox_attention",
    )(qi, kj, p3, p3, p3, cst, gain)


def _gdn_kernel(q_ref, k_ref, v_ref, z_ref, val_ref, cs_ref, cst_ref, gn_ref, o_ref, state_ref):
    @pl.when(pl.program_id(1) == 0)
    def _():
        state_ref[...] = jnp.zeros_like(state_ref)

    c = GDN_CHUNK
    n_chunks = q_ref.shape[1] // c
    row = lax.broadcasted_iota(I32, (c, c), 0)
    col = lax.broadcasted_iota(I32, (c, c), 1)
    incl = row >= col
    strict = row > col
    eye = jnp.where(row == col, 1.0, 0.0)
    pairs = [(ci, h) for ci in range(n_chunks) for h in range(GDN_HEADS)]

    def rows(ci):
        return slice(ci * c, (ci + 1) * c)

    def cols(h):
        return slice(h * HEAD_DIM, (h + 1) * HEAD_DIM)

    q, k, kb, decay, eg, gc, rhs = {}, {}, {}, {}, {}, {}, {}
    for ci, h in pairs:
        g = (ci, h)
        q[g] = q_ref[0, rows(ci), cols(h)]
        k[g] = k_ref[0, rows(ci), cols(h)]
        beta = val_ref[0, rows(ci), S_GB + h:S_GB + h + 1]
        gc[g] = cs_ref[0, rows(ci), S_GA + h:S_GA + h + 1]
        gr = cst_ref[0, S_GA + h:S_GA + h + 1, rows(ci)]
        decay[g] = jnp.where(incl, jnp.exp(jnp.where(incl, gc[g] - gr, 0.0)), 0.0)
        eg[g] = jnp.exp(gc[g])
        kb[g] = k[g] * beta
        rhs[g] = jnp.concatenate([v_ref[0, rows(ci), cols(h)] * beta, kb[g] * eg[g]], axis=1)
    a = {g: jnp.where(strict, _dot_nt(kb[g], k[g]) * decay[g], 0.0) for g in pairs}
    qk = {g: jnp.where(incl, _dot_nt(q[g], k[g]) * decay[g], 0.0) for g in pairs}
    off = (row // 2 == col // 2) & (row != col)
    tinv = {g: eye - jnp.where(off, a[g], 0.0) for g in pairs}
    sz = 2
    while sz < c:
        off = (row // (2 * sz) == col // (2 * sz)) & (row // sz != col // sz)
        left = {g: _dot(tinv[g], jnp.where(off, a[g], 0.0)) for g in pairs}
        tinv = {g: tinv[g] - _dot(left[g], tinv[g]) for g in pairs}
        sz *= 2
    sol = {g: _dot(tinv[g], rhs[g]) for g in pairs}
    state = [state_ref[h] for h in range(GDN_HEADS)]
    heads = range(GDN_HEADS)
    for ci in range(n_chunks):
        ws = [_dot(sol[ci, h][:, HEAD_DIM:], state[h]) for h in heads]
        qs = [_dot(q[ci, h] * eg[ci, h], state[h]) for h in heads]
        v_new = [sol[ci, h][:, :HEAD_DIM] - ws[h] for h in heads]
        intra = [_dot(qk[ci, h], v_new[h]) for h in heads]
        for h in heads:
            g_last = gc[ci, h][c - 1:c, :]
            k_dec = k[ci, h] * jnp.exp(g_last - gc[ci, h])
            state[h] = state[h] * jnp.exp(g_last) + _dot_tn(k_dec, v_new[h])
        for h in heads:
            o = _rms(qs[h] + intra[h], gn_ref[...]) * _silu(z_ref[0, rows(ci), cols(h)])
            o_ref[0, rows(ci), cols(h)] = o.astype(o_ref.dtype)
    for h in heads:
        state_ref[h] = state[h]


def _gdn(cv, p3, val, cs, cst, gain):
    b, s, _ = cv.shape
    t = _tile(s, GDN_ROWS)
    wide = lambda blk: pl.BlockSpec((1, t, GDN_W), lambda bi, i: (bi, i, blk))
    small = pl.BlockSpec((1, t, LANES), lambda bi, i: (bi, i, 0))
    return pl.pallas_call(
        _gdn_kernel,
        out_shape=jax.ShapeDtypeStruct((b, s, GDN_W), BF16),
        grid=(b, s // t),
        in_specs=[wide(C_GQ // GDN_W), wide(C_GK // GDN_W), wide(C_GV // GDN_W),
                  wide(P_GZ // GDN_W), small, small,
                  pl.BlockSpec((1, LANES, t), lambda bi, i: (bi, 0, i)),
                  pl.BlockSpec((1, HEAD_DIM), lambda bi, i: (0, 0))],
        out_specs=pl.BlockSpec((1, t, GDN_W), lambda bi, i: (bi, i, 0)),
        scratch_shapes=[pltpu.VMEM((GDN_HEADS, HEAD_DIM, HEAD_DIM), F32)],
        compiler_params=_params(("parallel", "arbitrary")),
        name="gated_deltanet",
    )(cv, cv, cv, p3, val, cs, cst, gain)


SSM_PAIRS = SSM_HEADS // 2
PAIRS_PER_GROUP = SSM_PAIRS // SSM_GROUPS


def _ssd_kernel(x_ref, b_ref, c_ref, z_ref, val_ref, cs_ref, cst_ref, par_ref, o_ref, st_ref):
    @pl.when(pl.program_id(1) == 0)
    def _():
        st_ref[...] = jnp.zeros_like(st_ref)

    n = SSM_CHUNK
    row = lax.broadcasted_iota(I32, (n, n), 0)
    col = lax.broadcasted_iota(I32, (n, n), 1)
    incl = row >= col
    lo = lax.broadcasted_iota(I32, (n, LANES), 1) < SSM_HEAD_DIM
    lo_row = lo[0:1, :]
    groups = range(SSM_GROUPS)
    pairs = range(SSM_PAIRS)
    lanes_of = lambda p: slice(p * LANES, (p + 1) * LANES)
    grp = lambda p: p // PAIRS_PER_GROUP
    st = [st_ref[p] for p in pairs]

    for ci in range(x_ref.shape[1] // n):
        rs = slice(ci * n, (ci + 1) * n)

        def decay_of(h):
            ac = cs_ref[0, rs, S_DT + h:S_DT + h + 1]
            ar = cst_ref[0, S_DT + h:S_DT + h + 1, rs]
            return jnp.where(incl, jnp.exp(jnp.where(incl, ac - ar, 0.0)), 0.0), ac

        bg = [b_ref[0, rs, g * SSM_STATE:(g + 1) * SSM_STATE] for g in groups]
        cg = [c_ref[0, rs, g * SSM_STATE:(g + 1) * SSM_STATE] for g in groups]
        cb = [_dot_nt(cg[g], bg[g]) for g in groups]
        bg_t = [bg[g].T for g in groups]
        xp, xdt, dec, e_ac, tail, st_dec = [], [], [], [], [], []
        for p in pairs:
            h0, h1 = 2 * p, 2 * p + 1
            xp.append(x_ref[0, rs, lanes_of(p)])
            dt = jnp.where(lo, val_ref[0, rs, S_DT + h0:S_DT + h0 + 1],
                           val_ref[0, rs, S_DT + h1:S_DT + h1 + 1])
            xdt.append(xp[p] * dt)
            d0, ac0 = decay_of(h0)
            d1, ac1 = decay_of(h1)
            dec.append((d0, d1))
            e_ac.append(jnp.where(lo, jnp.exp(ac0), jnp.exp(ac1)))
            al0 = ac0[n - 1:n, :]
            al1 = ac1[n - 1:n, :]
            tail.append(jnp.where(lo, jnp.exp(al0 - ac0), jnp.exp(al1 - ac1)) * xdt[p])
            st_dec.append(jnp.where(lo_row, jnp.exp(al0), jnp.exp(al1)))
        diag0 = [_dot(cb[grp(p)] * dec[p][0], xdt[p]) for p in pairs]
        diag1 = [_dot(cb[grp(p)] * dec[p][1], xdt[p]) for p in pairs]
        y_off = [_dot(cg[grp(p)], st[p]) for p in pairs]
        st_add = [_dot(bg_t[grp(p)], tail[p]) for p in pairs]
        ys = []
        for p in pairs:
            st[p] = st[p] * st_dec[p] + st_add[p]
            y_diag = jnp.where(lo, diag0[p], diag1[p])
            ys.append((y_diag + y_off[p] * e_ac[p] + par_ref[0:1, lanes_of(p)] * xp[p])
                      * _silu(z_ref[0, rs, lanes_of(p)]))
        for g in groups:
            mine = [p for p in pairs if grp(p) == g]
            ssq = sum(jnp.sum(ys[p] * ys[p], axis=-1, keepdims=True) for p in mine)
            inv = lax.rsqrt(ssq / (PAIRS_PER_GROUP * LANES) + NORM_EPS)
            for p in mine:
                o_ref[0, rs, lanes_of(p)] = (ys[p] * inv * par_ref[1:2, lanes_of(p)]).astype(o_ref.dtype)
    for p in pairs:
        st_ref[p] = st[p]


def _ssd(cv, p3, val, cs, cst, par):
    b, s, _ = cv.shape
    t = _tile(s, SSD_ROWS)
    small = pl.BlockSpec((1, t, LANES), lambda bi, i: (bi, i, 0))
    return pl.pallas_call(
        _ssd_kernel,
        out_shape=jax.ShapeDtypeStruct((b, s, SSM_W), BF16),
        grid=(b, s // t),
        in_specs=[pl.BlockSpec((1, t, SSM_W), lambda bi, i: (bi, i, C_SX // SSM_W)),
                  pl.BlockSpec((1, t, BC_W), lambda bi, i: (bi, i, C_SB // BC_W)),
                  pl.BlockSpec((1, t, BC_W), lambda bi, i: (bi, i, C_SC // BC_W)),
                  pl.BlockSpec((1, t, SSM_W), lambda bi, i: (bi, i, P_SZ // SSM_W)),
                  small, small,
                  pl.BlockSpec((1, LANES, t), lambda bi, i: (bi, 0, i)),
                  pl.BlockSpec((8, SSM_W), lambda bi, i: (0, 0))],
        out_specs=pl.BlockSpec((1, t, SSM_W), lambda bi, i: (bi, i, 0)),
        scratch_shapes=[pltpu.VMEM((SSM_PAIRS, SSM_STATE, LANES), F32)],
        compiler_params=_params(("parallel", "arbitrary")),
        name="ssd_scan",
    )(cv, cv, cv, p3, val, cs, cst, par)


def _outproj_kernel(a1_ref, a2_ref, a3_ref, w_ref, x_ref, o_ref):
    r1 = a1_ref.shape[1]
    r2 = r1 + a2_ref.shape[1]
    o_ref[...] = (x_ref[...] + _dot(a1_ref[...], w_ref[0, :r1, :]) + _dot(a2_ref[...], w_ref[0, r1:r2, :])
                  + _dot(a3_ref[...], w_ref[0, r2:, :]))


def _outproj(a1, a2, a3, w, layer, x):
    n, d = x.shape
    tm = _tile(n, 1024)
    tn = _tile(d, 1024)
    act = lambda a: pl.BlockSpec((tm, a.shape[1]), lambda i, j: (i, 0))
    return pl.pallas_call(
        _outproj_kernel,
        out_shape=jax.ShapeDtypeStruct((n, d), F32),
        grid=(n // tm, d // tn),
        in_specs=[act(a1), act(a2), act(a3),
                  pl.BlockSpec((1, w.shape[1], tn), lambda i, j: (layer, 0, j)),
                  pl.BlockSpec((tm, tn), lambda i, j: (i, j))],
        out_specs=pl.BlockSpec((tm, tn), lambda i, j: (i, j)),
        compiler_params=_params(("parallel", "parallel")),
        name="out_proj",
    )(a1, a2, a3, w, x)


def _ffn_kernel(x_ref, g_ref, wg_ref, wu_ref, wd_ref, o_ref, h_ref):
    @pl.when(pl.program_id(1) == 0)
    def _():
        x = x_ref[...]
        h_ref[...] = _rms(x, g_ref[...]).astype(BF16)
        o_ref[...] = x

    h = h_ref[...]
    act = _silu(_dot(h, wg_ref[0])) * _dot(h, wu_ref[0])
    o_ref[...] += _dot(act.astype(BF16), wd_ref[0])


def _ffn(x, gain, wg, wu, wd, layer):
    n, d = x.shape
    f = wg.shape[2]
    tm = _tile(n, 1024)
    tf = _tile(f, 512)
    return pl.pallas_call(
        _ffn_kernel,
        out_shape=jax.ShapeDtypeStruct((n, d), F32),
        grid=(n // tm, f // tf),
        in_specs=[pl.BlockSpec((tm, d), lambda i, j: (i, 0)),
                  pl.BlockSpec((1, d), lambda i, j: (0, 0)),
                  pl.BlockSpec((1, d, tf), lambda i, j: (layer, 0, j)),
                  pl.BlockSpec((1, d, tf), lambda i, j: (layer, 0, j)),
                  pl.BlockSpec((1, tf, d), lambda i, j: (layer, j, 0))],
        out_specs=pl.BlockSpec((tm, d), lambda i, j: (i, 0)),
        scratch_shapes=[pltpu.VMEM((tm, d), BF16)],
        compiler_params=_params(("parallel", "arbitrary")),
        name="dense_ffn",
    )(x, gain, wg, wu, wd)


def _router_kernel(x_ref, g_ref, r_ref, idx_ref, w_ref):
    h = _rms(x_ref[...], g_ref[...])
    r = r_ref[...]
    h_hi = h.astype(BF16)
    r_hi = r.astype(BF16)
    h_lo = (h - h_hi.astype(F32)).astype(BF16)
    r_lo = (r - r_hi.astype(F32)).astype(BF16)
    logits = _dot(h_hi, r_hi) + (_dot(h_hi, r_lo) + _dot(h_lo, r_hi))
    lane = lax.broadcasted_iota(I32, logits.shape, 1)
    l1 = jnp.where(lane < N_EXPERTS, logits, NEG_BIG)
    m1 = jnp.max(l1, axis=-1, keepdims=True)
    i1 = jnp.min(jnp.where(l1 == m1, lane, LANES), axis=-1, keepdims=True)
    l2 = jnp.where(lane == i1, NEG_BIG, l1)
    m2 = jnp.max(l2, axis=-1, keepdims=True)
    i2 = jnp.min(jnp.where(l2 == m2, lane, LANES), axis=-1, keepdims=True)
    e = jnp.exp(m2 - m1)
    idx_ref[...] = jnp.where(lane == 0, i1, jnp.where(lane == 1, i2, 0))
    w_ref[...] = jnp.where(lane == 0, 1.0 / (1.0 + e), jnp.where(lane == 1, e / (1.0 + e), 0.0))


def _router(x, gain, router_pad):
    n, d = x.shape
    tm = _tile(n, 512)
    return pl.pallas_call(
        _router_kernel,
        out_shape=(jax.ShapeDtypeStruct((n, LANES), I32), jax.ShapeDtypeStruct((n, LANES), F32)),
        grid=(n // tm,),
        in_specs=[pl.BlockSpec((tm, d), lambda i: (i, 0)),
                  pl.BlockSpec((1, d), lambda i: (0, 0)),
                  pl.BlockSpec((d, LANES), lambda i: (0, 0))],
        out_specs=(pl.BlockSpec((tm, LANES), lambda i: (i, 0)),
                   pl.BlockSpec((tm, LANES), lambda i: (i, 0))),
        compiler_params=_params(("parallel",)),
        name="moe_router",
    )(x, gain, router_pad)


def _row_copy(src_hbm, src_row, dst, dst_row, sem):
    return pltpu.make_async_copy(src_hbm.at[pl.ds(src_row, 1)], dst.at[pl.ds(dst_row, 1)], sem)


DMA_UNROLL = 8


def _dispatch_kernel(d0_ref, d1_ref, pad_ref, x_ref, g_ref, xs_hbm, hbuf, sems):
    i = pl.program_id(0)
    last = pl.num_programs(0) - 1
    rows = x_ref.shape[0]
    n_fill = pad_ref.shape[2]
    slot = lax.rem(i, 2)

    def wait_block(s):
        def body(r, carry):
            _row_copy(hbuf.at[s], r, xs_hbm, 0, sems.at[s]).wait()
            _row_copy(hbuf.at[s], r, xs_hbm, 0, sems.at[s]).wait()
            return carry

        def fill_body(r, carry):
            _row_copy(hbuf.at[s], 0, xs_hbm, 0, sems.at[s]).wait()
            return carry
        lax.fori_loop(0, rows, body, 0, unroll=DMA_UNROLL)
        lax.fori_loop(0, n_fill, fill_body, 0, unroll=DMA_UNROLL)

    @pl.when(i >= 2)
    def _():
        wait_block(slot)

    hbuf[slot] = _rms(x_ref[...], g_ref[...])

    def start(r, carry):
        _row_copy(hbuf.at[slot], r, xs_hbm, d0_ref[0, 0, r], sems.at[slot]).start(priority=0)
        _row_copy(hbuf.at[slot], r, xs_hbm, d1_ref[0, 0, r], sems.at[slot]).start(priority=1)
        return carry

    def start_fill(r, carry):
        _row_copy(hbuf.at[slot], 0, xs_hbm, pad_ref[0, 0, r], sems.at[slot]).start()
        return carry

    lax.fori_loop(0, rows, start, 0, unroll=DMA_UNROLL)
    lax.fori_loop(0, n_fill, start_fill, 0, unroll=DMA_UNROLL)

    @pl.when(i == last)
    def _():
        wait_block(slot)

    @pl.when((i == last) & (i >= 1))
    def _():
        wait_block(1 - slot)


def _dispatch(x, gain, dest0, dest1, pad_slots, n_slots):
    n, d = x.shape
    tb = _tile(n, 256)
    nb = n // tb
    n_fill = pad_slots.shape[0] // nb
    assert n_fill * nb == pad_slots.shape[0]
    dspec = pl.BlockSpec((1, 1, tb), lambda i: (i, 0, 0), memory_space=pltpu.SMEM)
    return pl.pallas_call(
        _dispatch_kernel,
        out_shape=jax.ShapeDtypeStruct((n_slots, d), F32),
        grid=(nb,),
        in_specs=[dspec, dspec,
                  pl.BlockSpec((1, 1, n_fill), lambda i: (i, 0, 0), memory_space=pltpu.SMEM),
                  pl.BlockSpec((tb, d), lambda i: (i, 0)),
                  pl.BlockSpec((1, d), lambda i: (0, 0))],
        out_specs=pl.BlockSpec(memory_space=pl.ANY),
        scratch_shapes=[pltpu.VMEM((2, tb, d), F32), pltpu.SemaphoreType.DMA((2,))],
        compiler_params=_params(("arbitrary",)),
        name="moe_dispatch",
    )(dest0.reshape(nb, 1, tb), dest1.reshape(nb, 1, tb), pad_slots.reshape(nb, 1, n_fill), x, gain)


def _expert_ffn_kernel(e_ref, na_ref, x_ref, wg_ref, wu_ref, wd_ref, o_ref, xb_ref):
    s = pl.program_id(0)
    f = pl.program_id(1)

    @pl.when(f == 0)
    def _():
        o_ref[...] = jnp.zeros_like(o_ref)
        xb_ref[...] = x_ref[...].astype(BF16)

    @pl.when(s < na_ref[0])
    def _():
        x = xb_ref[...]
        gate = _dot(x, wg_ref[0, 0].astype(BF16))
        up = _dot(x, wu_ref[0, 0].astype(BF16))
        o_ref[...] += _dot((_silu(gate) * up).astype(BF16), wd_ref[0, 0].astype(BF16))


def _expert_ffn(xs, blk_expert, n_active, wg, wu, wd, layer):
    n_slots, d = xs.shape
    f = wg.shape[3]
    nblk = n_slots // MOE_ROWS
    tf = _tile(f, 256)
    nf = f // tf

    def fidx(s, j, na_ref):
        return jnp.where(s < na_ref[0], j, nf - 1)

    grid_spec = pltpu.PrefetchScalarGridSpec(
        num_scalar_prefetch=2,
        grid=(nblk, nf),
        in_specs=[pl.BlockSpec((MOE_ROWS, d), lambda s, j, e, na: (s, 0)),
                  pl.BlockSpec((1, 1, d, tf), lambda s, j, e, na: (layer, e[s], 0, fidx(s, j, na))),
                  pl.BlockSpec((1, 1, d, tf), lambda s, j, e, na: (layer, e[s], 0, fidx(s, j, na))),
                  pl.BlockSpec((1, 1, tf, d), lambda s, j, e, na: (layer, e[s], fidx(s, j, na), 0))],
        out_specs=pl.BlockSpec((MOE_ROWS, d), lambda s, j, e, na: (s, 0)),
        scratch_shapes=[pltpu.VMEM((MOE_ROWS, d), BF16)],
    )
    return pl.pallas_call(
        _expert_ffn_kernel,
        out_shape=jax.ShapeDtypeStruct((n_slots, d), F32),
        grid_spec=grid_spec,
        compiler_params=_params(("arbitrary", "arbitrary")),
        name="moe_expert_ffn",
    )(blk_expert, n_active, xs, wg, wu, wd)


def _combine_kernel(d0_ref, d1_ref, d0_next_ref, d1_next_ref, x_ref, w_ref, g_ref, y_hbm, o_ref,
                    buf, sems, *, final_norm):
    i = pl.program_id(0)
    rows = o_ref.shape[0]
    slot = lax.rem(i, 2)

    def start_block(a_ref, b_ref, s):
        def body(r, carry):
            _row_copy(y_hbm, a_ref[0, 0, r], buf.at[s, 0], r, sems.at[s]).start(priority=0)
            _row_copy(y_hbm, b_ref[0, 0, r], buf.at[s, 1], r, sems.at[s]).start(priority=1)
            return carry
        lax.fori_loop(0, rows, body, 0, unroll=DMA_UNROLL)

    @pl.when(i == 0)
    def _():
        start_block(d0_ref, d1_ref, 0)

    @pl.when(i + 1 < pl.num_programs(0))
    def _():
        start_block(d0_next_ref, d1_next_ref, 1 - slot)

    def wait(r, carry):
        _row_copy(y_hbm, 0, buf.at[slot, 0], r, sems.at[slot]).wait()
        _row_copy(y_hbm, 0, buf.at[slot, 1], r, sems.at[slot]).wait()
        return carry

    lax.fori_loop(0, rows, wait, 0, unroll=DMA_UNROLL)
    out = x_ref[...] + w_ref[:, 0:1] * buf[slot, 0] + w_ref[:, 1:2] * buf[slot, 1]
    o_ref[...] = _rms(out, g_ref[...]) if final_norm else out


def _combine(x, top_w, dest0, dest1, ys, final_gain=None):
    n, d = x.shape
    tb = _tile(n, 256)
    nb = n // tb
    dspec = pl.BlockSpec((1, 1, tb), lambda i: (i, 0, 0), memory_space=pltpu.SMEM)
    dnext = pl.BlockSpec((1, 1, tb), lambda i: (jnp.minimum(i + 1, nb - 1), 0, 0),
                         memory_space=pltpu.SMEM)
    d0 = dest0.reshape(nb, 1, tb)
    d1 = dest1.reshape(nb, 1, tb)
    gain = jnp.ones((1, d), F32) if final_gain is None else final_gain
    return pl.pallas_call(
        functools.partial(_combine_kernel, final_norm=final_gain is not None),
        out_shape=jax.ShapeDtypeStruct((n, d), F32),
        grid=(nb,),
        in_specs=[dspec, dspec, dnext, dnext,
                  pl.BlockSpec((tb, d), lambda i: (i, 0)),
                  pl.BlockSpec((tb, LANES), lambda i: (i, 0)),
                  pl.BlockSpec((1, d), lambda i: (0, 0)),
                  pl.BlockSpec(memory_space=pl.ANY)],
        out_specs=pl.BlockSpec((tb, d), lambda i: (i, 0)),
        scratch_shapes=[pltpu.VMEM((2, 2, tb, d), F32), pltpu.SemaphoreType.DMA((2,))],
        compiler_params=_params(("arbitrary",)),
        name="moe_combine",
    )(d0, d1, d0, d1, x, top_w, gain, ys)


def _moe(x, gain, router, wg, wu, wd, layer, final_gain=None):
    n, d = x.shape
    router_pad = jnp.pad(router, ((0, 0), (0, LANES - N_EXPERTS)))
    top_idx, top_w = _router(x, gain, router_pad)
    flat_e = top_idx[:, :TOP_K].reshape(-1)
    onehot = (flat_e[:, None] == jnp.arange(N_EXPERTS, dtype=I32)[None, :]).astype(I32)
    rank = jnp.take_along_axis(jnp.cumsum(onehot, axis=0) - onehot, flat_e[:, None], axis=1)[:, 0]
    counts = jnp.sum(onehot, axis=0)
    padded = (counts + MOE_ROWS - 1) // MOE_ROWS * MOE_ROWS
    pad_end = jnp.cumsum(padded)
    dest = (pad_end - padded)[flat_e] + rank
    nblk = (n * TOP_K) // MOE_ROWS + N_EXPERTS
    n_slots = nblk * MOE_ROWS
    n_active = (pad_end[-1] // MOE_ROWS).astype(I32)
    blk_start = jnp.arange(nblk, dtype=I32) * MOE_ROWS
    count_le = lambda edges, v: jnp.sum((edges[None, :] <= v[:, None]).astype(I32), axis=1)
    blk_expert = jnp.minimum(count_le(pad_end, blk_start), N_EXPERTS - 1)
    last_e = blk_expert[jnp.maximum(n_active - 1, 0)]
    blk_expert = jnp.where(jnp.arange(nblk) < n_active, blk_expert, last_e).astype(I32)

    pad_sizes = jnp.concatenate([padded - counts, (n_slots - pad_end[-1])[None]])
    pad_first = jnp.concatenate([pad_end - padded + counts, pad_end[-1:]])
    pad_cum = jnp.cumsum(pad_sizes)
    k = jnp.arange(n_slots - n * TOP_K, dtype=I32)
    seg = count_le(pad_cum, k)
    pad_slots = (pad_first[seg] + k - (pad_cum - pad_sizes)[seg]).astype(I32)

    dest2 = dest.reshape(n, TOP_K)
    xs = _dispatch(x, gain, dest2[:, 0], dest2[:, 1], pad_slots, n_slots)
    ys = _expert_ffn(xs, blk_expert, n_active.reshape(1), wg, wu, wd, layer)
    return _combine(x, top_w, dest2[:, 0], dest2[:, 1], ys, final_gain)


def _final_norm_kernel(x_ref, g_ref, o_ref):
    o_ref[...] = _rms(x_ref[...], g_ref[...])


def _final_norm(x, gain):
    n, d = x.shape
    tm = _tile(n, 512)
    return pl.pallas_call(
        _final_norm_kernel,
        out_shape=jax.ShapeDtypeStruct((n, d), F32),
        grid=(n // tm,),
        in_specs=[pl.BlockSpec((tm, d), lambda i: (i, 0)), pl.BlockSpec((1, d), lambda i: (0, 0))],
        out_specs=pl.BlockSpec((tm, d), lambda i: (i, 0)),
        compiler_params=_params(("parallel",)),
        name="final_norm",
    )(x, gain)


def _layout_w_in(w_in):
    offs = [0]
    for sz in PROJ_SIZES:
        offs.append(offs[-1] + sz)
    seg = [w_in[..., offs[k]:offs[k + 1]] for k in range(len(PROJ_SIZES))]
    fq, fk, fv, ff, gqkv, gz, gb, ga, sz_, sxbc, sdt = seg
    lead = w_in.shape[:-1]
    small = jnp.concatenate([ff, gb, ga, sdt, jnp.zeros(lead + (LANES - S_END,), w_in.dtype)], -1)
    tail = jnp.zeros(lead + (P_WIDTH - P_SMALL - LANES,), w_in.dtype)
    return jnp.concatenate([fq, fk, fv, gqkv, gz, sz_, sxbc, small, tail], -1).astype(BF16)


def _row_tile(rows, width):
    out = jnp.zeros((8, width), F32)
    for r, v in enumerate(rows):
        out = out.at[r, :v.shape[0]].set(v.astype(F32))
    return out


def kernel(x, norm_mix, w_in, fox_f_bias, fox_out_norm, gdn_conv_w, gdn_A_log, gdn_dt_bias,
           gdn_out_norm, ssm_conv_w, ssm_conv_b, ssm_A_log, ssm_dt_bias, ssm_D, ssm_out_norm,
           w_out, norm_ffn, ffn_w_gate, ffn_w_up, ffn_w_down, moe_router, moe_w_gate,
           moe_w_up, moe_w_down, norm_final):
    b, s, d = x.shape
    depth = w_in.shape[0]
    n = b * s
    w_in_l = _layout_w_in(w_in)
    w_out_b = w_out.astype(BF16)
    ffn_g, ffn_u, ffn_d = (w.astype(BF16) for w in (ffn_w_gate, ffn_w_up, ffn_w_down))
    conv_w = jnp.concatenate([gdn_conv_w, ssm_conv_w], axis=-1)
    conv_b = jnp.concatenate([jnp.zeros((depth, 3 * GDN_W), F32), ssm_conv_b], axis=-1)

    xf = x.reshape(n, d)
    for layer in range(depth):
        p = _norm_matmul(xf, norm_mix[layer][None, :], w_in_l, layer)
        p3 = p.reshape(b, s, P_WIDTH)
        zeros = lambda k: jnp.zeros((k,), F32)
        bias_row = jnp.concatenate([fox_f_bias[layer], zeros(GDN_HEADS), gdn_dt_bias[layer],
                                    ssm_dt_bias[layer]])
        alog_row = jnp.concatenate([zeros(S_GA), gdn_A_log[layer], ssm_A_log[layer]])
        val, cs, cst = _prep(p3, _row_tile([bias_row, alog_row], LANES))
        cv = _conv(p3, conv_w[layer], conv_b[layer][None, :])
        o_fox = _fox(p3, cst, fox_out_norm[layer].reshape(1, FOX_W))
        o_gdn = _gdn(cv, p3, val, cs, cst, gdn_out_norm[layer][None, :])
        ssm_par = _row_tile([jnp.repeat(ssm_D[layer], SSM_HEAD_DIM), ssm_out_norm[layer]], SSM_W)
        o_ssm = _ssd(cv, p3, val, cs, cst, ssm_par)
        xf = _outproj(o_fox.reshape(n, FOX_W), o_gdn.reshape(n, GDN_W), o_ssm.reshape(n, SSM_W),
                      w_out_b, layer, xf)
        j = layer // 2
        gain = norm_ffn[layer][None, :]
        if layer % 2 == 0:
            xf = _ffn(xf, gain, ffn_g, ffn_u, ffn_d, j)
        else:
            fused_final = norm_final[None, :] if layer == depth - 1 else None
            xf = _moe(xf, gain, moe_router[j], moe_w_gate, moe_w_up, moe_w_down, j, fused_final)
    if depth % 2:
        xf = _final_norm(xf, norm_final[None, :])
    return xf.reshape(b, s, d)
```

```python
import functools

import jax
import jax.numpy as jnp
from jax import lax
from jax.experimental import pallas as pl
from jax.experimental.pallas import tpu as pltpu

F32 = jnp.float32
BF16 = jnp.bfloat16
I32 = jnp.int32

NORM_EPS = 1e-6
LANES = 128
NEG_BIG = -1e30

FOX_HEADS = 4
HEAD_DIM = 128
GDN_HEADS = 6
GDN_CHUNK = 64
GDN_ROWS = 256
SSM_HEADS = 12
SSM_HEAD_DIM = 64
SSM_STATE = 128
SSM_GROUPS = 2
SSM_CHUNK = 128
SSD_ROWS = 256
PREP_ROWS = 512
CONV_WIDTH = 4
N_EXPERTS = 8
TOP_K = 2

FOX_W = FOX_HEADS * HEAD_DIM
GDN_W = GDN_HEADS * HEAD_DIM
SSM_W = SSM_HEADS * SSM_HEAD_DIM
BC_W = SSM_GROUPS * SSM_STATE
PROJ_SIZES = (FOX_W, FOX_W, FOX_W, FOX_HEADS, 3 * GDN_W, GDN_W, GDN_HEADS, GDN_HEADS,
              SSM_W, SSM_W + 2 * BC_W, SSM_HEADS)

P_FQ, P_FK, P_FV = 0, FOX_W, 2 * FOX_W
P_GQKV = 3 * FOX_W
P_GZ = P_GQKV + 3 * GDN_W
P_SZ = P_GZ + GDN_W
P_SXBC = P_SZ + SSM_W
P_SMALL = P_SXBC + SSM_W + 2 * BC_W
P_WIDTH = 6912
S_FF, S_GB, S_GA, S_DT = 0, 4, 10, 16
S_END = S_DT + SSM_HEADS
C_GQ, C_GK, C_GV = 0, GDN_W, 2 * GDN_W
C_SX = 3 * GDN_W
C_SB = C_SX + SSM_W
C_SC = C_SB + BC_W
C_WIDTH = C_SC + BC_W

MOE_ROWS = 1024
VMEM_LIMIT = 56 * 1024 * 1024


def _tile(n, pref):
    t = min(n, pref)
    while n % t:
        t //= 2
    return t


def _params(sem, vmem=VMEM_LIMIT):
    return pltpu.CompilerParams(dimension_semantics=sem, vmem_limit_bytes=vmem)


def _silu(x):
    return x / (1.0 + jnp.exp(-x))


def _softplus(x):
    return jnp.maximum(x, 0.0) + jnp.log1p(jnp.exp(-jnp.abs(x)))


def _rms(x, gain):
    return x * lax.rsqrt(jnp.mean(x * x, axis=-1, keepdims=True) + NORM_EPS) * gain


def _dot(a, b):
    return jnp.dot(a, b, preferred_element_type=F32)


def _dot_nt(a, b):
    return lax.dot_general(a, b, (((1,), (1,)), ((), ())), preferred_element_type=F32)


def _dot_tn(a, b):
    return lax.dot_general(a, b, (((0,), (0,)), ((), ())), preferred_element_type=F32)


def _norm_matmul_kernel(x_ref, g_ref, w_ref, o_ref, h_ref):
    @pl.when(pl.program_id(1) == 0)
    def _():
        h_ref[...] = _rms(x_ref[...], g_ref[...]).astype(BF16)

    o_ref[...] = _dot(h_ref[...], w_ref[0])


def _norm_matmul(x, gain, w, layer):
    n, d = x.shape
    nout = w.shape[2]
    tm = _tile(n, 1024)
    tn = _tile(nout, 768)
    return pl.pallas_call(
        _norm_matmul_kernel,
        out_shape=jax.ShapeDtypeStruct((n, nout), F32),
        grid=(n // tm, nout // tn),
        in_specs=[pl.BlockSpec((tm, d), lambda i, j: (i, 0)),
                  pl.BlockSpec((1, d), lambda i, j: (0, 0)),
                  pl.BlockSpec((1, d, tn), lambda i, j: (layer, 0, j))],
        out_specs=pl.BlockSpec((tm, tn), lambda i, j: (i, j)),
        scratch_shapes=[pltpu.VMEM((tm, d), BF16)],
        compiler_params=_params(("parallel", "arbitrary")),
        name="norm_inproj",
    )(x, gain, w)


def _prep_kernel(p_ref, par_ref, val_ref, cs_ref, cst_ref, carry_ref):
    @pl.when(pl.program_id(1) == 0)
    def _():
        carry_ref[...] = jnp.zeros_like(carry_ref)

    blk = SSM_CHUNK
    lane = lax.broadcasted_iota(I32, (blk, LANES), 1)
    row = lax.broadcasted_iota(I32, (blk, blk), 0)
    col = lax.broadcasted_iota(I32, (blk, blk), 1)
    tri = row >= col
    tri_blk = jnp.where(tri, 1.0, 0.0)
    tri_gdn = jnp.where(tri & (row // GDN_CHUNK == col // GDN_CHUNK), 1.0, 0.0)
    neg_a = -jnp.exp(par_ref[1:2, :])
    carry = carry_ref[...]
    for sb in range(p_ref.shape[1] // blk):
        rs = slice(sb * blk, (sb + 1) * blk)
        v = p_ref[0, rs, :] + par_ref[0:1, :]
        sp = _softplus(v)
        log_f = -_softplus(-v)
        beta = 1.0 / (1.0 + jnp.exp(-v))
        val_ref[0, rs, :] = jnp.where(lane < S_GA, beta, sp)
        z = jnp.where(lane < S_GB, log_f, jnp.where(lane < S_GA, 0.0, neg_a * sp))
        z = jnp.where(lane < S_END, z, 0.0)
        cs_blk = jnp.dot(tri_blk, z, precision=lax.Precision.HIGHEST, preferred_element_type=F32)
        cs_gdn = jnp.dot(tri_gdn, z, precision=lax.Precision.HIGHEST, preferred_element_type=F32)
        cs_run = cs_blk + carry
        carry = cs_run[blk - 1:blk, :]
        cs = jnp.where(lane < S_GB, cs_run, jnp.where(lane < S_DT, cs_gdn, cs_blk))
        cs_ref[0, rs, :] = cs
        cst_ref[0, :, rs] = cs.T
    carry_ref[...] = carry


def _prep(p3, par):
    b, s, _ = p3.shape
    blk = _tile(s, PREP_ROWS)
    shp = jax.ShapeDtypeStruct((b, s, LANES), F32)
    return pl.pallas_call(
        _prep_kernel,
        out_shape=(shp, shp, jax.ShapeDtypeStruct((b, LANES, s), F32)),
        grid=(b, s // blk),
        in_specs=[pl.BlockSpec((1, blk, LANES), lambda bi, i: (bi, i, P_SMALL // LANES)),
                  pl.BlockSpec((8, LANES), lambda bi, i: (0, 0))],
        out_specs=(pl.BlockSpec((1, blk, LANES), lambda bi, i: (bi, i, 0)),
                   pl.BlockSpec((1, blk, LANES), lambda bi, i: (bi, i, 0)),
                   pl.BlockSpec((1, LANES, blk), lambda bi, i: (bi, 0, i))),
        scratch_shapes=[pltpu.VMEM((1, LANES), F32)],
        compiler_params=_params(("parallel", "arbitrary")),
        name="gate_prep",
    )(p3, par)


CONV_COLS = 256
CONV_GDN_BLOCKS = 3 * GDN_W // CONV_COLS
CONV_L2_BLOCKS = 2 * GDN_W // CONV_COLS
CONV_Q_BLOCKS = GDN_W // CONV_COLS


CONV_STRIP = 64


def _conv_kernel(u_ref, halo_ref, w_ref, b_ref, o_ref, edge_ref):
    i = pl.program_id(1)
    c = pl.program_id(2)
    t = u_ref.shape[1]
    strip = min(CONV_STRIP, t)
    edge_ref[0:8, :] = jnp.where(i > 0, halo_ref[0], 0.0)
    edge_ref[8:, :] = u_ref[0, 0:strip, :]
    taps = [w_ref[k:k + 1, :] for k in range(CONV_WIDTH)]
    bias = b_ref[...]

    def conv_strip(r0):
        window = edge_ref[...] if r0 == 0 else u_ref[0, r0 - 8:r0 + strip, :]
        acc = bias + window[8:] * taps[CONV_WIDTH - 1]
        for back in range(1, CONV_WIDTH):
            acc = acc + pltpu.roll(window, back, 0)[8:] * taps[CONV_WIDTH - 1 - back]
        return _silu(acc)

    @pl.when(c < CONV_L2_BLOCKS)
    def _():
        scale = jnp.where(c < CONV_Q_BLOCKS, HEAD_DIM ** -0.5, 1.0)
        for r0 in range(0, t, strip):
            y = conv_strip(r0)
            for k in range(CONV_COLS // HEAD_DIM):
                yk = y[:, k * HEAD_DIM:(k + 1) * HEAD_DIM]
                inv = lax.rsqrt(jnp.sum(yk * yk, axis=-1, keepdims=True) + NORM_EPS) * scale
                o_ref[0, r0:r0 + strip, k * HEAD_DIM:(k + 1) * HEAD_DIM] = yk * inv

    @pl.when(c >= CONV_L2_BLOCKS)
    def _():
        for r0 in range(0, t, strip):
            o_ref[0, r0:r0 + strip, :] = conv_strip(r0)


def _conv(p3, cw, cb):
    b, s, _ = p3.shape
    t = _tile(s, 1024)
    gdn0 = P_GQKV // CONV_COLS
    ssm_shift = P_SXBC // CONV_COLS - CONV_GDN_BLOCKS

    def col(c):
        return jnp.where(c < CONV_GDN_BLOCKS, c + gdn0, c + ssm_shift)

    return pl.pallas_call(
        _conv_kernel,
        out_shape=jax.ShapeDtypeStruct((b, s, C_WIDTH), F32),
        grid=(b, s // t, C_WIDTH // CONV_COLS),
        in_specs=[pl.BlockSpec((1, t, CONV_COLS), lambda bi, i, c: (bi, i, col(c))),
                  pl.BlockSpec((1, 8, CONV_COLS),
                               lambda bi, i, c: (bi, jnp.maximum(i * (t // 8) - 1, 0), col(c))),
                  pl.BlockSpec((CONV_WIDTH, CONV_COLS), lambda bi, i, c: (0, c)),
                  pl.BlockSpec((1, CONV_COLS), lambda bi, i, c: (0, c))],
        out_specs=pl.BlockSpec((1, t, CONV_COLS), lambda bi, i, c: (bi, i, c)),
        scratch_shapes=[pltpu.VMEM((8 + min(CONV_STRIP, t), CONV_COLS), F32)],
        compiler_params=_params(("parallel", "parallel", "parallel")),
        name="conv_silu",
    )(p3, p3, cw, cb)


def _fox_kernel(qi_ref, kj_ref, q_ref, k_ref, v_ref, ck_ref, gn_ref, o_ref, m_ref, acc_ref):
    i = qi_ref[pl.program_id(1)]
    j = kj_ref[pl.program_id(1)]
    tq = q_ref.shape[1]
    tk = k_ref.shape[1]
    cols = lambda h: slice(h * HEAD_DIM, (h + 1) * HEAD_DIM)
    wide = lambda h: slice(2 * h * HEAD_DIM, 2 * (h + 1) * HEAD_DIM)

    @pl.when(j == 0)
    def _():
        m_ref[...] = jnp.full_like(m_ref, NEG_BIG)
        acc_ref[...] = jnp.zeros_like(acc_ref)

    def step(masked):
        if masked:
            causal = (lax.broadcasted_iota(I32, (tq, tk), 1) <= lax.broadcasted_iota(I32, (tq, tk), 0))
        ones = jnp.ones((tk, HEAD_DIM), BF16)
        s, p, alpha = {}, {}, {}

        def logits(h):
            q = (q_ref[0, :, cols(h)] * HEAD_DIM ** -0.5).astype(BF16)
            sh = _dot_nt(q, k_ref[0, :, cols(h)].astype(BF16)) - ck_ref[0, h:h + 1, :]
            s[h] = jnp.where(causal, sh, NEG_BIG) if masked else sh

        def softmax(h):
            m_prev = m_ref[h]
            m_new = jnp.maximum(m_prev, jnp.max(s[h], axis=-1, keepdims=True))
            p[h] = jnp.exp(s[h] - jnp.tile(m_new, (1, tk // LANES))).astype(BF16)
            alpha[h] = jnp.exp(m_prev - m_new)
            m_ref[h] = m_new

        def values(h):
            v1 = jnp.concatenate([v_ref[0, :, cols(h)].astype(BF16), ones], axis=1)
            acc_ref[:, wide(h)] = jnp.tile(alpha[h], (1, 2)) * acc_ref[:, wide(h)] + _dot(p[h], v1)

        for t in range(FOX_HEADS + 2):
            if t < FOX_HEADS:
                logits(t)
            if 0 <= t - 1 < FOX_HEADS:
                softmax(t - 1)
            if 0 <= t - 2 < FOX_HEADS:
                values(t - 2)

    @pl.when(j < i)
    def _():
        step(False)

    @pl.when(j == i)
    def _():
        step(True)
        for h in range(FOX_HEADS):
            both = acc_ref[:, wide(h)]
            o = both[:, :HEAD_DIM] / both[:, HEAD_DIM:]
            o_ref[0, :, cols(h)] = _rms(o, gn_ref[:, cols(h)]).astype(o_ref.dtype)


def _fox(p3, cst, gain):
    b, s, _ = p3.shape
    t = _tile(s, 512)
    n = s // t
    pairs = [(i, j) for i in range(n) for j in range(i + 1)]
    qi = jnp.asarray([p[0] for p in pairs], I32)
    kj = jnp.asarray([p[1] for p in pairs], I32)
    grid_spec = pltpu.PrefetchScalarGridSpec(
        num_scalar_prefetch=2,
        grid=(b, len(pairs)),
        in_specs=[pl.BlockSpec((1, t, FOX_W), lambda bi, p, qi, kj: (bi, qi[p], P_FQ // FOX_W)),
                  pl.BlockSpec((1, t, FOX_W), lambda bi, p, qi, kj: (bi, kj[p], P_FK // FOX_W)),
                  pl.BlockSpec((1, t, FOX_W), lambda bi, p, qi, kj: (bi, kj[p], P_FV // FOX_W)),
                  pl.BlockSpec((1, 8, t), lambda bi, p, qi, kj: (bi, 0, kj[p])),
                  pl.BlockSpec((1, FOX_W), lambda bi, p, qi, kj: (0, 0))],
        out_specs=pl.BlockSpec((1, t, FOX_W), lambda bi, p, qi, kj: (bi, qi[p], 0)),
        scratch_shapes=[pltpu.VMEM((FOX_HEADS, t, LANES), F32),
                        pltpu.VMEM((t, 2 * FOX_W), F32)],
    )
    return pl.pallas_call(
        _fox_kernel,
        out_shape=jax.ShapeDtypeStruct((b, s, FOX_W), BF16),
        grid_spec=grid_spec,
        compiler_params=_params(("parallel", "arbitrary")),
        name="fox_attention",
    )(qi, kj, p3, p3, p3, cst, gain)


def _gdn_kernel(q_ref, k_ref, v_ref, z_ref, val_ref, cs_ref, cst_ref, gn_ref, o_ref, state_ref):
    @pl.when(pl.program_id(1) == 0)
    def _():
        state_ref[...] = jnp.zeros_like(state_ref)

    c = GDN_CHUNK
    n_chunks = q_ref.shape[1] // c
    row = lax.broadcasted_iota(I32, (c, c), 0)
    col = lax.broadcasted_iota(I32, (c, c), 1)
    incl = row >= col
    strict = row > col
    eye = jnp.where(row == col, 1.0, 0.0)
    pairs = [(ci, h) for ci in range(n_chunks) for h in range(GDN_HEADS)]

    def rows(ci):
        return slice(ci * c, (ci + 1) * c)

    def cols(h):
        return slice(h * HEAD_DIM, (h + 1) * HEAD_DIM)

    q, k, kb, decay, eg, gc, rhs = {}, {}, {}, {}, {}, {}, {}
    for ci, h in pairs:
        g = (ci, h)
        q[g] = q_ref[0, rows(ci), cols(h)]
        k[g] = k_ref[0, rows(ci), cols(h)]
        beta = jnp.broadcast_to(val_ref[0, rows(ci), S_GB + h:S_GB + h + 1], (c, HEAD_DIM))
        gc[g] = jnp.broadcast_to(cs_ref[0, rows(ci), S_GA + h:S_GA + h + 1], (c, HEAD_DIM))
        gr = cst_ref[0, S_GA + h:S_GA + h + 1, rows(ci)]
        decay[g] = jnp.where(incl, jnp.exp(jnp.where(incl, gc[g][:, :c] - gr, 0.0)), 0.0)
        eg[g] = jnp.exp(gc[g])
        kb[g] = k[g] * beta
        rhs[g] = jnp.concatenate([v_ref[0, rows(ci), cols(h)] * beta, kb[g] * eg[g]], axis=1)
    a = {g: jnp.where(strict, _dot_nt(kb[g], k[g]) * decay[g], 0.0) for g in pairs}
    qk = {g: jnp.where(incl, _dot_nt(q[g], k[g]) * decay[g], 0.0) for g in pairs}
    off = (row // 2 == col // 2) & (row != col)
    tinv = {g: eye - jnp.where(off, a[g], 0.0) for g in pairs}
    sz = 2
    while sz < c:
        off = (row // (2 * sz) == col // (2 * sz)) & (row // sz != col // sz)
        left = {g: _dot(tinv[g], jnp.where(off, a[g], 0.0)) for g in pairs}
        tinv = {g: tinv[g] - _dot(left[g], tinv[g]) for g in pairs}
        sz *= 2
    sol = {g: _dot(tinv[g], rhs[g]) for g in pairs}
    state = [state_ref[h] for h in range(GDN_HEADS)]
    heads = range(GDN_HEADS)
    for ci in range(n_chunks):
        ws = [_dot(sol[ci, h][:, HEAD_DIM:], state[h]) for h in heads]
        qs = [_dot(q[ci, h] * eg[ci, h], state[h]) for h in heads]
        v_new = [sol[ci, h][:, :HEAD_DIM] - ws[h] for h in heads]
        intra = [_dot(qk[ci, h], v_new[h]) for h in heads]
        for h in heads:
            g_last = gc[ci, h][c - 1:c, :]
            k_dec = k[ci, h] * jnp.exp(g_last - gc[ci, h])
            state[h] = state[h] * jnp.exp(g_last) + _dot_tn(k_dec, v_new[h])
        for h in heads:
            o = _rms(qs[h] + intra[h], gn_ref[...]) * _silu(z_ref[0, rows(ci), cols(h)])
            o_ref[0, rows(ci), cols(h)] = o.astype(o_ref.dtype)
    for h in heads:
        state_ref[h] = state[h]


def _gdn(cv, p3, val, cs, cst, gain):
    b, s, _ = cv.shape
    t = _tile(s, GDN_ROWS)
    wide = lambda blk: pl.BlockSpec((1, t, GDN_W), lambda bi, i: (bi, i, blk))
    small = pl.BlockSpec((1, t, LANES), lambda bi, i: (bi, i, 0))
    return pl.pallas_call(
        _gdn_kernel,
        out_shape=jax.ShapeDtypeStruct((b, s, GDN_W), BF16),
        grid=(b, s // t),
        in_specs=[wide(C_GQ // GDN_W), wide(C_GK // GDN_W), wide(C_GV // GDN_W),
                  wide(P_GZ // GDN_W), small, small,
                  pl.BlockSpec((1, LANES, t), lambda bi, i: (bi, 0, i)),
                  pl.BlockSpec((1, HEAD_DIM), lambda bi, i: (0, 0))],
        out_specs=pl.BlockSpec((1, t, GDN_W), lambda bi, i: (bi, i, 0)),
        scratch_shapes=[pltpu.VMEM((GDN_HEADS, HEAD_DIM, HEAD_DIM), F32)],
        compiler_params=_params(("parallel", "arbitrary")),
        name="gated_deltanet",
    )(cv, cv, cv, p3, val, cs, cst, gain)


SSM_PAIRS = SSM_HEADS // 2
PAIRS_PER_GROUP = SSM_PAIRS // SSM_GROUPS


def _ssd_kernel(x_ref, b_ref, c_ref, z_ref, val_ref, cs_ref, cst_ref, par_ref, o_ref, st_ref):
    @pl.when(pl.program_id(1) == 0)
    def _():
        st_ref[...] = jnp.zeros_like(st_ref)

    n = SSM_CHUNK
    row = lax.broadcasted_iota(I32, (n, n), 0)
    col = lax.broadcasted_iota(I32, (n, n), 1)
    incl = row >= col
    lo = lax.broadcasted_iota(I32, (n, LANES), 1) < SSM_HEAD_DIM
    groups = range(SSM_GROUPS)
    pairs = range(SSM_PAIRS)
    lanes_of = lambda p: slice(p * LANES, (p + 1) * LANES)
    grp = lambda p: p // PAIRS_PER_GROUP
    st = [st_ref[p] for p in pairs]

    for ci in range(x_ref.shape[1] // n):
        rs = slice(ci * n, (ci + 1) * n)

        def decay_of(h):
            ac = jnp.broadcast_to(cs_ref[0, rs, S_DT + h:S_DT + h + 1], (n, LANES))
            ar = cst_ref[0, S_DT + h:S_DT + h + 1, rs]
            return jnp.where(incl, jnp.exp(jnp.where(incl, ac - ar, 0.0)), 0.0), ac

        bg = [b_ref[0, rs, g * SSM_STATE:(g + 1) * SSM_STATE] for g in groups]
        cg = [c_ref[0, rs, g * SSM_STATE:(g + 1) * SSM_STATE] for g in groups]
        cb = [_dot_nt(cg[g], bg[g]) for g in groups]
        bg_t = [bg[g].T for g in groups]
        xp, xdt, dec, e_ac, tail, st_dec = [], [], [], [], [], []
        for p in pairs:
            h0, h1 = 2 * p, 2 * p + 1
            xp.append(x_ref[0, rs, lanes_of(p)])
            dt = jnp.where(lo, val_ref[0, rs, S_DT + h0:S_DT + h0 + 1],
                           val_ref[0, rs, S_DT + h1:S_DT + h1 + 1])
            xdt.append(xp[p] * dt)
            d0, ac0 = decay_of(h0)
            d1, ac1 = decay_of(h1)
            dec.append((d0, d1))
            ac = jnp.where(lo, ac0, ac1)
            ac_last = ac[n - 1:n, :]
            e_ac.append(jnp.exp(ac))
            tail.append(jnp.exp(ac_last - ac) * xdt[p])
            st_dec.append(jnp.exp(ac_last))
        diag0 = [_dot(cb[grp(p)] * dec[p][0], xdt[p]) for p in pairs]
        diag1 = [_dot(cb[grp(p)] * dec[p][1], xdt[p]) for p in pairs]
        y_off = [_dot(cg[grp(p)], st[p]) for p in pairs]
        st_add = [_dot(bg_t[grp(p)], tail[p]) for p in pairs]
        ys = []
        for p in pairs:
            st[p] = st[p] * st_dec[p] + st_add[p]
            y_diag = jnp.where(lo, diag0[p], diag1[p])
            ys.append((y_diag + y_off[p] * e_ac[p] + par_ref[0:1, lanes_of(p)] * xp[p])
                      * _silu(z_ref[0, rs, lanes_of(p)]))
        for g in groups:
            mine = [p for p in pairs if grp(p) == g]
            ssq = sum(jnp.sum(ys[p] * ys[p], axis=-1, keepdims=True) for p in mine)
            inv = lax.rsqrt(ssq / (PAIRS_PER_GROUP * LANES) + NORM_EPS)
            for p in mine:
                o_ref[0, rs, lanes_of(p)] = (ys[p] * inv * par_ref[1:2, lanes_of(p)]).astype(o_ref.dtype)
    for p in pairs:
        st_ref[p] = st[p]


def _ssd(cv, p3, val, cs, cst, par):
    b, s, _ = cv.shape
    t = _tile(s, SSD_ROWS)
    small = pl.BlockSpec((1, t, LANES), lambda bi, i: (bi, i, 0))
    return pl.pallas_call(
        _ssd_kernel,
        out_shape=jax.ShapeDtypeStruct((b, s, SSM_W), BF16),
        grid=(b, s // t),
        in_specs=[pl.BlockSpec((1, t, SSM_W), lambda bi, i: (bi, i, C_SX // SSM_W)),
                  pl.BlockSpec((1, t, BC_W), lambda bi, i: (bi, i, C_SB // BC_W)),
                  pl.BlockSpec((1, t, BC_W), lambda bi, i: (bi, i, C_SC // BC_W)),
                  pl.BlockSpec((1, t, SSM_W), lambda bi, i: (bi, i, P_SZ // SSM_W)),
                  small, small,
                  pl.BlockSpec((1, LANES, t), lambda bi, i: (bi, 0, i)),
                  pl.BlockSpec((8, SSM_W), lambda bi, i: (0, 0))],
        out_specs=pl.BlockSpec((1, t, SSM_W), lambda bi, i: (bi, i, 0)),
        scratch_shapes=[pltpu.VMEM((SSM_PAIRS, SSM_STATE, LANES), F32)],
        compiler_params=_params(("parallel", "arbitrary")),
        name="ssd_scan",
    )(cv, cv, cv, p3, val, cs, cst, par)


def _outproj_kernel(a1_ref, a2_ref, a3_ref, w_ref, x_ref, o_ref):
    r1 = a1_ref.shape[1]
    r2 = r1 + a2_ref.shape[1]
    o_ref[...] = (x_ref[...] + _dot(a1_ref[...], w_ref[0, :r1, :]) + _dot(a2_ref[...], w_ref[0, r1:r2, :])
                  + _dot(a3_ref[...], w_ref[0, r2:, :]))


def _outproj(a1, a2, a3, w, layer, x):
    n, d = x.shape
    tm = _tile(n, 1024)
    tn = _tile(d, 1024)
    act = lambda a: pl.BlockSpec((tm, a.shape[1]), lambda i, j: (i, 0))
    return pl.pallas_call(
        _outproj_kernel,
        out_shape=jax.ShapeDtypeStruct((n, d), F32),
        grid=(n // tm, d // tn),
        in_specs=[act(a1), act(a2), act(a3),
                  pl.BlockSpec((1, w.shape[1], tn), lambda i, j: (layer, 0, j)),
                  pl.BlockSpec((tm, tn), lambda i, j: (i, j))],
        out_specs=pl.BlockSpec((tm, tn), lambda i, j: (i, j)),
        compiler_params=_params(("parallel", "parallel")),
        name="out_proj",
    )(a1, a2, a3, w, x)


def _ffn_kernel(x_ref, g_ref, wg_ref, wu_ref, wd_ref, o_ref, h_ref):
    @pl.when(pl.program_id(1) == 0)
    def _():
        x = x_ref[...]
        h_ref[...] = _rms(x, g_ref[...]).astype(BF16)
        o_ref[...] = x

    h = h_ref[...]
    act = _silu(_dot(h, wg_ref[0])) * _dot(h, wu_ref[0])
    o_ref[...] += _dot(act.astype(BF16), wd_ref[0])


def _ffn(x, gain, wg, wu, wd, layer):
    n, d = x.shape
    f = wg.shape[2]
    tm = _tile(n, 1024)
    tf = _tile(f, 512)
    return pl.pallas_call(
        _ffn_kernel,
        out_shape=jax.ShapeDtypeStruct((n, d), F32),
        grid=(n // tm, f // tf),
        in_specs=[pl.BlockSpec((tm, d), lambda i, j: (i, 0)),
                  pl.BlockSpec((1, d), lambda i, j: (0, 0)),
                  pl.BlockSpec((1, d, tf), lambda i, j: (layer, 0, j)),
                  pl.BlockSpec((1, d, tf), lambda i, j: (layer, 0, j)),
                  pl.BlockSpec((1, tf, d), lambda i, j: (layer, j, 0))],
        out_specs=pl.BlockSpec((tm, d), lambda i, j: (i, 0)),
        scratch_shapes=[pltpu.VMEM((tm, d), BF16)],
        compiler_params=_params(("parallel", "arbitrary")),
        name="dense_ffn",
    )(x, gain, wg, wu, wd)


def _router_kernel(x_ref, g_ref, r_ref, idx_ref, w_ref):
    h = _rms(x_ref[...], g_ref[...])
    r = r_ref[...]
    h_hi = h.astype(BF16)
    r_hi = r.astype(BF16)
    h_lo = (h - h_hi.astype(F32)).astype(BF16)
    r_lo = (r - r_hi.astype(F32)).astype(BF16)
    logits = _dot(h_hi, r_hi) + (_dot(h_hi, r_lo) + _dot(h_lo, r_hi))
    lane = lax.broadcasted_iota(I32, logits.shape, 1)
    l1 = jnp.where(lane < N_EXPERTS, logits, NEG_BIG)
    m1 = jnp.max(l1, axis=-1, keepdims=True)
    i1 = jnp.min(jnp.where(l1 == m1, lane, LANES), axis=-1, keepdims=True)
    l2 = jnp.where(lane == i1, NEG_BIG, l1)
    m2 = jnp.max(l2, axis=-1, keepdims=True)
    i2 = jnp.min(jnp.where(l2 == m2, lane, LANES), axis=-1, keepdims=True)
    e = jnp.exp(m2 - m1)
    idx_ref[...] = jnp.where(lane == 0, i1, jnp.where(lane == 1, i2, 0))
    w_ref[...] = jnp.where(lane == 0, 1.0 / (1.0 + e), jnp.where(lane == 1, e / (1.0 + e), 0.0))


def _router(x, gain, router_pad):
    n, d = x.shape
    tm = _tile(n, 512)
    return pl.pallas_call(
        _router_kernel,
        out_shape=(jax.ShapeDtypeStruct((n, LANES), I32), jax.ShapeDtypeStruct((n, LANES), F32)),
        grid=(n // tm,),
        in_specs=[pl.BlockSpec((tm, d), lambda i: (i, 0)),
                  pl.BlockSpec((1, d), lambda i: (0, 0)),
                  pl.BlockSpec((d, LANES), lambda i: (0, 0))],
        out_specs=(pl.BlockSpec((tm, LANES), lambda i: (i, 0)),
                   pl.BlockSpec((tm, LANES), lambda i: (i, 0))),
        compiler_params=_params(("parallel",)),
        name="moe_router",
    )(x, gain, router_pad)


def _row_copy(src_hbm, src_row, dst, dst_row, sem):
    return pltpu.make_async_copy(src_hbm.at[pl.ds(src_row, 1)], dst.at[pl.ds(dst_row, 1)], sem)


DMA_UNROLL = 8


def _for_rows(n, fn):
    assert n % DMA_UNROLL == 0

    def group(g, carry):
        base = pl.multiple_of(g * DMA_UNROLL, DMA_UNROLL)
        for u in range(DMA_UNROLL):
            fn(base + u)
        return carry

    lax.fori_loop(0, n // DMA_UNROLL, group, 0)


def _dispatch_kernel(d0_ref, d1_ref, pad_ref, x_ref, g_ref, xs_hbm, hbuf, sems):
    i = pl.program_id(0)
    last = pl.num_programs(0) - 1
    rows = x_ref.shape[0]
    n_fill = pad_ref.shape[2]
    slot = lax.rem(i, 2)

    def wait_block(s):
        def body(r):
            _row_copy(hbuf.at[s], r, xs_hbm, 0, sems.at[s]).wait()
            _row_copy(hbuf.at[s], r, xs_hbm, 0, sems.at[s]).wait()

        def fill_body(r):
            _row_copy(hbuf.at[s], 0, xs_hbm, 0, sems.at[s]).wait()
        _for_rows(rows, body)
        _for_rows(n_fill, fill_body)

    @pl.when(i >= 2)
    def _():
        wait_block(slot)

    hbuf[slot] = _rms(x_ref[...], g_ref[...])

    def start(r):
        _row_copy(hbuf.at[slot], r, xs_hbm, d0_ref[0, 0, r], sems.at[slot]).start()
        _row_copy(hbuf.at[slot], r, xs_hbm, d1_ref[0, 0, r], sems.at[slot]).start()

    def start_fill(r):
        _row_copy(hbuf.at[slot], 0, xs_hbm, pad_ref[0, 0, r], sems.at[slot]).start()

    _for_rows(rows, start)
    _for_rows(n_fill, start_fill)

    @pl.when(i == last)
    def _():
        wait_block(slot)

    @pl.when((i == last) & (i >= 1))
    def _():
        wait_block(1 - slot)


def _dispatch(x, gain, dest0, dest1, pad_slots, n_slots):
    n, d = x.shape
    tb = _tile(n, 256)
    nb = n // tb
    n_fill = pad_slots.shape[0] // nb
    assert n_fill * nb == pad_slots.shape[0]
    dspec = pl.BlockSpec((1, 1, tb), lambda i: (i, 0, 0), memory_space=pltpu.SMEM)
    return pl.pallas_call(
        _dispatch_kernel,
        out_shape=jax.ShapeDtypeStruct((n_slots, d), F32),
        grid=(nb,),
        in_specs=[dspec, dspec,
                  pl.BlockSpec((1, 1, n_fill), lambda i: (i, 0, 0), memory_space=pltpu.SMEM),
                  pl.BlockSpec((tb, d), lambda i: (i, 0)),
                  pl.BlockSpec((1, d), lambda i: (0, 0))],
        out_specs=pl.BlockSpec(memory_space=pl.ANY),
        scratch_shapes=[pltpu.VMEM((2, tb, d), F32), pltpu.SemaphoreType.DMA((2,))],
        compiler_params=_params(("arbitrary",)),
        name="moe_dispatch",
    )(dest0.reshape(nb, 1, tb), dest1.reshape(nb, 1, tb), pad_slots.reshape(nb, 1, n_fill), x, gain)


def _expert_ffn_kernel(e_ref, na_ref, x_ref, wg_ref, wu_ref, wd_ref, o_ref, xb_ref):
    s = pl.program_id(0)
    f = pl.program_id(1)

    @pl.when(f == 0)
    def _():
        o_ref[...] = jnp.zeros_like(o_ref)
        xb_ref[...] = x_ref[...].astype(BF16)

    @pl.when(s < na_ref[0])
    def _():
        x = xb_ref[...]
        gate = _dot(x, wg_ref[0, 0].astype(BF16))
        up = _dot(x, wu_ref[0, 0].astype(BF16))
        o_ref[...] += _dot((_silu(gate) * up).astype(BF16), wd_ref[0, 0].astype(BF16))


def _expert_ffn(xs, blk_expert, n_active, wg, wu, wd, layer):
    n_slots, d = xs.shape
    f = wg.shape[3]
    nblk = n_slots // MOE_ROWS
    tf = _tile(f, 256)
    nf = f // tf

    def fidx(s, j, na_ref):
        return jnp.where(s < na_ref[0], j, nf - 1)

    grid_spec = pltpu.PrefetchScalarGridSpec(
        num_scalar_prefetch=2,
        grid=(nblk, nf),
        in_specs=[pl.BlockSpec((MOE_ROWS, d), lambda s, j, e, na: (s, 0)),
                  pl.BlockSpec((1, 1, d, tf), lambda s, j, e, na: (layer, e[s], 0, fidx(s, j, na))),
                  pl.BlockSpec((1, 1, d, tf), lambda s, j, e, na: (layer, e[s], 0, fidx(s, j, na))),
                  pl.BlockSpec((1, 1, tf, d), lambda s, j, e, na: (layer, e[s], fidx(s, j, na), 0))],
        out_specs=pl.BlockSpec((MOE_ROWS, d), lambda s, j, e, na: (s, 0)),
        scratch_shapes=[pltpu.VMEM((MOE_ROWS, d), BF16)],
    )
    return pl.pallas_call(
        _expert_ffn_kernel,
        out_shape=jax.ShapeDtypeStruct((n_slots, d), F32),
        grid_spec=grid_spec,
        compiler_params=_params(("arbitrary", "arbitrary")),
        name="moe_expert_ffn",
    )(blk_expert, n_active, xs, wg, wu, wd)


def _combine_kernel(d0_ref, d1_ref, d0_next_ref, d1_next_ref, x_ref, w_ref, g_ref, y_hbm, o_ref,
                    buf, sems, *, final_norm):
    i = pl.program_id(0)
    rows = o_ref.shape[0]
    slot = lax.rem(i, 2)

    def start_block(a_ref, b_ref, s):
        def body(r):
            _row_copy(y_hbm, a_ref[0, 0, r], buf.at[s, 0], r, sems.at[s]).start()
            _row_copy(y_hbm, b_ref[0, 0, r], buf.at[s, 1], r, sems.at[s]).start()
        _for_rows(rows, body)

    @pl.when(i == 0)
    def _():
        start_block(d0_ref, d1_ref, 0)

    @pl.when(i + 1 < pl.num_programs(0))
    def _():
        start_block(d0_next_ref, d1_next_ref, 1 - slot)

    def wait(r):
        _row_copy(y_hbm, 0, buf.at[slot, 0], r, sems.at[slot]).wait()
        _row_copy(y_hbm, 0, buf.at[slot, 1], r, sems.at[slot]).wait()

    _for_rows(rows, wait)
    out = x_ref[...] + w_ref[:, 0:1] * buf[slot, 0] + w_ref[:, 1:2] * buf[slot, 1]
    o_ref[...] = _rms(out, g_ref[...]) if final_norm else out


def _combine(x, top_w, dest0, dest1, ys, final_gain=None):
    n, d = x.shape
    tb = _tile(n, 256)
    nb = n // tb
    dspec = pl.BlockSpec((1, 1, tb), lambda i: (i, 0, 0), memory_space=pltpu.SMEM)
    dnext = pl.BlockSpec((1, 1, tb), lambda i: (jnp.minimum(i + 1, nb - 1), 0, 0),
                         memory_space=pltpu.SMEM)
    d0 = dest0.reshape(nb, 1, tb)
    d1 = dest1.reshape(nb, 1, tb)
    gain = jnp.ones((1, d), F32) if final_gain is None else final_gain
    return pl.pallas_call(
        functools.partial(_combine_kernel, final_norm=final_gain is not None),
        out_shape=jax.ShapeDtypeStruct((n, d), F32),
        grid=(nb,),
        in_specs=[dspec, dspec, dnext, dnext,
                  pl.BlockSpec((tb, d), lambda i: (i, 0)),
                  pl.BlockSpec((tb, LANES), lambda i: (i, 0)),
                  pl.BlockSpec((1, d), lambda i: (0, 0)),
                  pl.BlockSpec(memory_space=pl.ANY)],
        out_specs=pl.BlockSpec((tb, d), lambda i: (i, 0)),
        scratch_shapes=[pltpu.VMEM((2, 2, tb, d), F32), pltpu.SemaphoreType.DMA((2,))],
        compiler_params=_params(("arbitrary",)),
        name="moe_combine",
    )(d0, d1, d0, d1, x, top_w, gain, ys)


def _moe(x, gain, router, wg, wu, wd, layer, final_gain=None):
    n, d = x.shape
    router_pad = jnp.pad(router, ((0, 0), (0, LANES - N_EXPERTS)))
    top_idx, top_w = _router(x, gain, router_pad)
    flat_e = top_idx[:, :TOP_K].reshape(-1)
    onehot = (flat_e[:, None] == jnp.arange(N_EXPERTS, dtype=I32)[None, :]).astype(I32)
    rank = jnp.take_along_axis(jnp.cumsum(onehot, axis=0) - onehot, flat_e[:, None], axis=1)[:, 0]
    counts = jnp.sum(onehot, axis=0)
    padded = (counts + MOE_ROWS - 1) // MOE_ROWS * MOE_ROWS
    pad_end = jnp.cumsum(padded)
    dest = (pad_end - padded)[flat_e] + rank
    nblk = (n * TOP_K) // MOE_ROWS + N_EXPERTS
    n_slots = nblk * MOE_ROWS
    n_active = (pad_end[-1] // MOE_ROWS).astype(I32)
    blk_start = jnp.arange(nblk, dtype=I32) * MOE_ROWS
    count_le = lambda edges, v: jnp.sum((edges[None, :] <= v[:, None]).astype(I32), axis=1)
    blk_expert = jnp.minimum(count_le(pad_end, blk_start), N_EXPERTS - 1)
    last_e = blk_expert[jnp.maximum(n_active - 1, 0)]
    blk_expert = jnp.where(jnp.arange(nblk) < n_active, blk_expert, last_e).astype(I32)

    pad_sizes = jnp.concatenate([padded - counts, (n_slots - pad_end[-1])[None]])
    pad_first = jnp.concatenate([pad_end - padded + counts, pad_end[-1:]])
    pad_cum = jnp.cumsum(pad_sizes)
    k = jnp.arange(n_slots - n * TOP_K, dtype=I32)
    seg = count_le(pad_cum, k)
    pad_slots = (pad_first[seg] + k - (pad_cum - pad_sizes)[seg]).astype(I32)

    dest2 = dest.reshape(n, TOP_K)
    xs = _dispatch(x, gain, dest2[:, 0], dest2[:, 1], pad_slots, n_slots)
    ys = _expert_ffn(xs, blk_expert, n_active.reshape(1), wg, wu, wd, layer)
    return _combine(x, top_w, dest2[:, 0], dest2[:, 1], ys, final_gain)


def _final_norm_kernel(x_ref, g_ref, o_ref):
    o_ref[...] = _rms(x_ref[...], g_ref[...])


def _final_norm(x, gain):
    n, d = x.shape
    tm = _tile(n, 512)
    return pl.pallas_call(
        _final_norm_kernel,
        out_shape=jax.ShapeDtypeStruct((n, d), F32),
        grid=(n // tm,),
        in_specs=[pl.BlockSpec((tm, d), lambda i: (i, 0)), pl.BlockSpec((1, d), lambda i: (0, 0))],
        out_specs=pl.BlockSpec((tm, d), lambda i: (i, 0)),
        compiler_params=_params(("parallel",)),
        name="final_norm",
    )(x, gain)


def _layout_w_in(w_in):
    offs = [0]
    for sz in PROJ_SIZES:
        offs.append(offs[-1] + sz)
    seg = [w_in[..., offs[k]:offs[k + 1]] for k in range(len(PROJ_SIZES))]
    fq, fk, fv, ff, gqkv, gz, gb, ga, sz_, sxbc, sdt = seg
    lead = w_in.shape[:-1]
    small = jnp.concatenate([ff, gb, ga, sdt, jnp.zeros(lead + (LANES - S_END,), w_in.dtype)], -1)
    tail = jnp.zeros(lead + (P_WIDTH - P_SMALL - LANES,), w_in.dtype)
    return jnp.concatenate([fq, fk, fv, gqkv, gz, sz_, sxbc, small, tail], -1).astype(BF16)


def _row_tile(rows, width):
    out = jnp.zeros((8, width), F32)
    for r, v in enumerate(rows):
        out = out.at[r, :v.shape[0]].set(v.astype(F32))
    return out


def kernel(x, norm_mix, w_in, fox_f_bias, fox_out_norm, gdn_conv_w, gdn_A_log, gdn_dt_bias,
           gdn_out_norm, ssm_conv_w, ssm_conv_b, ssm_A_log, ssm_dt_bias, ssm_D, ssm_out_norm,
           w_out, norm_ffn, ffn_w_gate, ffn_w_up, ffn_w_down, moe_router, moe_w_gate,
           moe_w_up, moe_w_down, norm_final):
    b, s, d = x.shape
    depth = w_in.shape[0]
    n = b * s
    w_in_l = _layout_w_in(w_in)
    w_out_b = w_out.astype(BF16)
    ffn_g, ffn_u, ffn_d = (w.astype(BF16) for w in (ffn_w_gate, ffn_w_up, ffn_w_down))
    conv_w = jnp.concatenate([gdn_conv_w, ssm_conv_w], axis=-1)
    conv_b = jnp.concatenate([jnp.zeros((depth, 3 * GDN_W), F32), ssm_conv_b], axis=-1)

    xf = x.reshape(n, d)
    for layer in range(depth):
        p = _norm_matmul(xf, norm_mix[layer][None, :], w_in_l, layer)
        p3 = p.reshape(b, s, P_WIDTH)
        zeros = lambda k: jnp.zeros((k,), F32)
        bias_row = jnp.concatenate([fox_f_bias[layer], zeros(GDN_HEADS), gdn_dt_bias[layer],
                                    ssm_dt_bias[layer]])
        alog_row = jnp.concatenate([zeros(S_GA), gdn_A_log[layer], ssm_A_log[layer]])
        val, cs, cst = _prep(p3, _row_tile([bias_row, alog_row], LANES))
        cv = _conv(p3, conv_w[layer], conv_b[layer][None, :])
        o_fox = _fox(p3, cst, fox_out_norm[layer].reshape(1, FOX_W))
        o_gdn = _gdn(cv, p3, val, cs, cst, gdn_out_norm[layer][None, :])
        ssm_par = _row_tile([jnp.repeat(ssm_D[layer], SSM_HEAD_DIM), ssm_out_norm[layer]], SSM_W)
        o_ssm = _ssd(cv, p3, val, cs, cst, ssm_par)
        xf = _outproj(o_fox.reshape(n, FOX_W), o_gdn.reshape(n, GDN_W), o_ssm.reshape(n, SSM_W),
                      w_out_b, layer, xf)
        j = layer // 2
        gain = norm_ffn[layer][None, :]
        if layer % 2 == 0:
            xf = _ffn(xf, gain, ffn_g, ffn_u, ffn_d, j)
        else:
            fused_final = norm_final[None, :] if layer == depth - 1 else None
            xf = _moe(xf, gain, moe_router[j], moe_w_gate, moe_w_up, moe_w_down, j, fused_final)
    if depth % 2:
        xf = _final_norm(xf, norm_final[None, :])
    return xf.reshape(b, s, d)
```

```python
import functools

import jax
import jax.numpy as jnp
from jax import lax
from jax.experimental import pallas as pl
from jax.experimental.pallas import tpu as pltpu

F32 = jnp.float32
BF16 = jnp.bfloat16
I32 = jnp.int32

NORM_EPS = 1e-6
LANES = 128
NEG_BIG = -1e30

FOX_HEADS = 4
HEAD_DIM = 128
GDN_HEADS = 6
GDN_CHUNK = 64
GDN_ROWS = 256
SSM_HEADS = 12
SSM_HEAD_DIM = 64
SSM_STATE = 128
SSM_GROUPS = 2
SSM_CHUNK = 128
SSD_ROWS = 256
PREP_ROWS = 512
CONV_WIDTH = 4
N_EXPERTS = 8
TOP_K = 2

FOX_W = FOX_HEADS * HEAD_DIM
GDN_W = GDN_HEADS * HEAD_DIM
SSM_W = SSM_HEADS * SSM_HEAD_DIM
BC_W = SSM_GROUPS * SSM_STATE
PROJ_SIZES = (FOX_W, FOX_W, FOX_W, FOX_HEADS, 3 * GDN_W, GDN_W, GDN_HEADS, GDN_HEADS,
              SSM_W, SSM_W + 2 * BC_W, SSM_HEADS)

P_FQ, P_FK, P_FV = 0, FOX_W, 2 * FOX_W
P_GQKV = 3 * FOX_W
P_GZ = P_GQKV + 3 * GDN_W
P_SZ = P_GZ + GDN_W
P_SXBC = P_SZ + SSM_W
P_SMALL = P_SXBC + SSM_W + 2 * BC_W
P_WIDTH = 6912
S_FF, S_GB, S_GA, S_DT = 0, 4, 10, 16
S_END = S_DT + SSM_HEADS
C_GQ, C_GK, C_GV = 0, GDN_W, 2 * GDN_W
C_SX = 3 * GDN_W
C_SB = C_SX + SSM_W
C_SC = C_SB + BC_W
C_WIDTH = C_SC + BC_W

MOE_ROWS = 1024
VMEM_LIMIT = 56 * 1024 * 1024


def _tile(n, pref):
    t = min(n, pref)
    while n % t:
        t //= 2
    return t


def _params(sem, vmem=VMEM_LIMIT):
    return pltpu.CompilerParams(dimension_semantics=sem, vmem_limit_bytes=vmem)


def _silu(x):
    return x / (1.0 + jnp.exp(-x))


def _softplus(x):
    return jnp.maximum(x, 0.0) + jnp.log1p(jnp.exp(-jnp.abs(x)))


def _rms(x, gain):
    return x * lax.rsqrt(jnp.mean(x * x, axis=-1, keepdims=True) + NORM_EPS) * gain


def _dot(a, b):
    return jnp.dot(a, b, preferred_element_type=F32)


def _dot_nt(a, b):
    return lax.dot_general(a, b, (((1,), (1,)), ((), ())), preferred_element_type=F32)


def _dot_tn(a, b):
    return lax.dot_general(a, b, (((0,), (0,)), ((), ())), preferred_element_type=F32)


def _norm_matmul_kernel(x_ref, g_ref, w_ref, o_ref, h_ref):
    @pl.when(pl.program_id(1) == 0)
    def _():
        h_ref[...] = _rms(x_ref[...], g_ref[...]).astype(BF16)

    o_ref[...] = _dot(h_ref[...], w_ref[0])


def _norm_matmul(x, gain, w, layer):
    n, d = x.shape
    nout = w.shape[2]
    tm = _tile(n, 1024)
    tn = _tile(nout, 768)
    return pl.pallas_call(
        _norm_matmul_kernel,
        out_shape=jax.ShapeDtypeStruct((n, nout), F32),
        grid=(n // tm, nout // tn),
        in_specs=[pl.BlockSpec((tm, d), lambda i, j: (i, 0)),
                  pl.BlockSpec((1, d), lambda i, j: (0, 0)),
                  pl.BlockSpec((1, d, tn), lambda i, j: (layer, 0, j))],
        out_specs=pl.BlockSpec((tm, tn), lambda i, j: (i, j)),
        scratch_shapes=[pltpu.VMEM((tm, d), BF16)],
        compiler_params=_params(("parallel", "arbitrary")),
        name="norm_inproj",
    )(x, gain, w)


def _prep_kernel(p_ref, par_ref, val_ref, cs_ref, cst_ref, carry_ref):
    @pl.when(pl.program_id(1) == 0)
    def _():
        carry_ref[...] = jnp.zeros_like(carry_ref)

    blk = SSM_CHUNK
    lane = lax.broadcasted_iota(I32, (blk, LANES), 1)
    row = lax.broadcasted_iota(I32, (blk, blk), 0)
    col = lax.broadcasted_iota(I32, (blk, blk), 1)
    tri = row >= col
    tri_blk = jnp.where(tri, 1.0, 0.0)
    tri_gdn = jnp.where(tri & (row // GDN_CHUNK == col // GDN_CHUNK), 1.0, 0.0)
    neg_a = -jnp.exp(par_ref[1:2, :])
    carry = carry_ref[...]
    for sb in range(p_ref.shape[1] // blk):
        rs = slice(sb * blk, (sb + 1) * blk)
        v = p_ref[0, rs, :] + par_ref[0:1, :]
        sp = _softplus(v)
        log_f = -_softplus(-v)
        beta = 1.0 / (1.0 + jnp.exp(-v))
        val_ref[0, rs, :] = jnp.where(lane < S_GA, beta, sp)
        z = jnp.where(lane < S_GB, log_f, jnp.where(lane < S_GA, 0.0, neg_a * sp))
        z = jnp.where(lane < S_END, z, 0.0)
        cs_blk = jnp.dot(tri_blk, z, precision=lax.Precision.HIGHEST, preferred_element_type=F32)
        cs_gdn = jnp.dot(tri_gdn, z, precision=lax.Precision.HIGHEST, preferred_element_type=F32)
        cs_run = cs_blk + carry
        carry = cs_run[blk - 1:blk, :]
        cs = jnp.where(lane < S_GB, cs_run, jnp.where(lane < S_DT, cs_gdn, cs_blk))
        cs_ref[0, rs, :] = cs
        cst_ref[0, :, rs] = cs.T
    carry_ref[...] = carry


def _prep(p3, par):
    b, s, _ = p3.shape
    blk = _tile(s, PREP_ROWS)
    shp = jax.ShapeDtypeStruct((b, s, LANES), F32)
    return pl.pallas_call(
        _prep_kernel,
        out_shape=(shp, shp, jax.ShapeDtypeStruct((b, LANES, s), F32)),
        grid=(b, s // blk),
        in_specs=[pl.BlockSpec((1, blk, LANES), lambda bi, i: (bi, i, P_SMALL // LANES)),
                  pl.BlockSpec((8, LANES), lambda bi, i: (0, 0))],
        out_specs=(pl.BlockSpec((1, blk, LANES), lambda bi, i: (bi, i, 0)),
                   pl.BlockSpec((1, blk, LANES), lambda bi, i: (bi, i, 0)),
                   pl.BlockSpec((1, LANES, blk), lambda bi, i: (bi, 0, i))),
        scratch_shapes=[pltpu.VMEM((1, LANES), F32)],
        compiler_params=_params(("parallel", "arbitrary")),
        name="gate_prep",
    )(p3, par)


CONV_COLS = 256
CONV_GDN_BLOCKS = 3 * GDN_W // CONV_COLS
CONV_L2_BLOCKS = 2 * GDN_W // CONV_COLS
CONV_Q_BLOCKS = GDN_W // CONV_COLS


CONV_STRIP = 64


def _conv_kernel(u_ref, halo_ref, w_ref, b_ref, o_ref, edge_ref):
    i = pl.program_id(1)
    c = pl.program_id(2)
    t = u_ref.shape[1]
    strip = min(CONV_STRIP, t)
    edge_ref[0:8, :] = jnp.where(i > 0, halo_ref[0], 0.0)
    edge_ref[8:, :] = u_ref[0, 0:strip, :]
    taps = [w_ref[k:k + 1, :] for k in range(CONV_WIDTH)]
    bias = b_ref[...]

    def conv_strip(r0):
        window = edge_ref[...] if r0 == 0 else u_ref[0, r0 - 8:r0 + strip, :]
        acc = bias + window[8:] * taps[CONV_WIDTH - 1]
        for back in range(1, CONV_WIDTH):
            acc = acc + pltpu.roll(window, back, 0)[8:] * taps[CONV_WIDTH - 1 - back]
        return _silu(acc)

    @pl.when(c < CONV_L2_BLOCKS)
    def _():
        scale = jnp.where(c < CONV_Q_BLOCKS, HEAD_DIM ** -0.5, 1.0)
        for r0 in range(0, t, strip):
            y = conv_strip(r0)
            for k in range(CONV_COLS // HEAD_DIM):
                yk = y[:, k * HEAD_DIM:(k + 1) * HEAD_DIM]
                inv = lax.rsqrt(jnp.sum(yk * yk, axis=-1, keepdims=True) + NORM_EPS) * scale
                o_ref[0, r0:r0 + strip, k * HEAD_DIM:(k + 1) * HEAD_DIM] = yk * inv

    @pl.when(c >= CONV_L2_BLOCKS)
    def _():
        for r0 in range(0, t, strip):
            o_ref[0, r0:r0 + strip, :] = conv_strip(r0)


def _conv(p3, cw, cb):
    b, s, _ = p3.shape
    t = _tile(s, 1024)
    gdn0 = P_GQKV // CONV_COLS
    ssm_shift = P_SXBC // CONV_COLS - CONV_GDN_BLOCKS

    def col(c):
        return jnp.where(c < CONV_GDN_BLOCKS, c + gdn0, c + ssm_shift)

    return pl.pallas_call(
        _conv_kernel,
        out_shape=jax.ShapeDtypeStruct((b, s, C_WIDTH), F32),
        grid=(b, s // t, C_WIDTH // CONV_COLS),
        in_specs=[pl.BlockSpec((1, t, CONV_COLS), lambda bi, i, c: (bi, i, col(c))),
                  pl.BlockSpec((1, 8, CONV_COLS),
                               lambda bi, i, c: (bi, jnp.maximum(i * (t // 8) - 1, 0), col(c))),
                  pl.BlockSpec((CONV_WIDTH, CONV_COLS), lambda bi, i, c: (0, c)),
                  pl.BlockSpec((1, CONV_COLS), lambda bi, i, c: (0, c))],
        out_specs=pl.BlockSpec((1, t, CONV_COLS), lambda bi, i, c: (bi, i, c)),
        scratch_shapes=[pltpu.VMEM((8 + min(CONV_STRIP, t), CONV_COLS), F32)],
        compiler_params=_params(("parallel", "parallel", "parallel")),
        name="conv_silu",
    )(p3, p3, cw, cb)


def _fox_kernel(qi_ref, kj_ref, q_ref, k_ref, v_ref, ck_ref, gn_ref, o_ref, m_ref, acc_ref):
    i = qi_ref[pl.program_id(1)]
    j = kj_ref[pl.program_id(1)]
    tq = q_ref.shape[1]
    tk = k_ref.shape[1]
    cols = lambda h: slice(h * HEAD_DIM, (h + 1) * HEAD_DIM)
    wide = lambda h: slice(2 * h * HEAD_DIM, 2 * (h + 1) * HEAD_DIM)

    @pl.when(j == 0)
    def _():
        m_ref[...] = jnp.full_like(m_ref, NEG_BIG)
        acc_ref[...] = jnp.zeros_like(acc_ref)

    def step(masked):
        if masked:
            causal = (lax.broadcasted_iota(I32, (tq, tk), 1) <= lax.broadcasted_iota(I32, (tq, tk), 0))
        ones = jnp.ones((tk, HEAD_DIM), BF16)
        s, p, alpha = {}, {}, {}

        def logits(h):
            q = (q_ref[0, :, cols(h)] * HEAD_DIM ** -0.5).astype(BF16)
            sh = _dot_nt(q, k_ref[0, :, cols(h)].astype(BF16)) - ck_ref[0, h:h + 1, :]
            s[h] = jnp.where(causal, sh, NEG_BIG) if masked else sh

        def softmax(h):
            m_prev = m_ref[h]
            m_new = jnp.maximum(m_prev, jnp.max(s[h], axis=-1, keepdims=True))
            p[h] = jnp.exp(s[h] - jnp.tile(m_new, (1, tk // LANES))).astype(BF16)
            alpha[h] = jnp.exp(m_prev - m_new)
            m_ref[h] = m_new

        def values(h):
            v1 = jnp.concatenate([v_ref[0, :, cols(h)].astype(BF16), ones], axis=1)
            acc_ref[:, wide(h)] = jnp.tile(alpha[h], (1, 2)) * acc_ref[:, wide(h)] + _dot(p[h], v1)

        for t in range(FOX_HEADS + 2):
            if t < FOX_HEADS:
                logits(t)
            if 0 <= t - 1 < FOX_HEADS:
                softmax(t - 1)
            if 0 <= t - 2 < FOX_HEADS:
                values(t - 2)

    @pl.when(j < i)
    def _():
        step(False)

    @pl.when(j == i)
    def _():
        step(True)
        for h in range(FOX_HEADS):
            both = acc_ref[:, wide(h)]
            o = both[:, :HEAD_DIM] / both[:, HEAD_DIM:]
            o_ref[0, :, cols(h)] = _rms(o, gn_ref[:, cols(h)]).astype(o_ref.dtype)


def _fox(p3, cst, gain):
    b, s, _ = p3.shape
    t = _tile(s, 512)
    n = s // t
    pairs = [(i, j) for i in range(n) for j in range(i + 1)]
    qi = jnp.asarray([p[0] for p in pairs], I32)
    kj = jnp.asarray([p[1] for p in pairs], I32)
    grid_spec = pltpu.PrefetchScalarGridSpec(
        num_scalar_prefetch=2,
        grid=(b, len(pairs)),
        in_specs=[pl.BlockSpec((1, t, FOX_W), lambda bi, p, qi, kj: (bi, qi[p], P_FQ // FOX_W)),
                  pl.BlockSpec((1, t, FOX_W), lambda bi, p, qi, kj: (bi, kj[p], P_FK // FOX_W)),
                  pl.BlockSpec((1, t, FOX_W), lambda bi, p, qi, kj: (bi, kj[p], P_FV // FOX_W)),
                  pl.BlockSpec((1, 8, t), lambda bi, p, qi, kj: (bi, 0, kj[p])),
                  pl.BlockSpec((1, FOX_W), lambda bi, p, qi, kj: (0, 0))],
        out_specs=pl.BlockSpec((1, t, FOX_W), lambda bi, p, qi, kj: (bi, qi[p], 0)),
        scratch_shapes=[pltpu.VMEM((FOX_HEADS, t, LANES), F32),
                        pltpu.VMEM((t, 2 * FOX_W), F32)],
    )
    return pl.pallas_call(
        _fox_kernel,
        out_shape=jax.ShapeDtypeStruct((b, s, FOX_W), BF16),
        grid_spec=grid_spec,
        compiler_params=_params(("parallel", "arbitrary")),
        name="fox_attention",
    )(qi, kj, p3, p3, p3, cst, gain)


def _gdn_kernel(q_ref, k_ref, v_ref, z_ref, val_ref, cs_ref, cst_ref, gn_ref, o_ref, state_ref):
    @pl.when(pl.program_id(1) == 0)
    def _():
        state_ref[...] = jnp.zeros_like(state_ref)

    c = GDN_CHUNK
    n_chunks = q_ref.shape[1] // c
    row = lax.broadcasted_iota(I32, (c, c), 0)
    col = lax.broadcasted_iota(I32, (c, c), 1)
    incl = row >= col
    strict = row > col
    eye = jnp.where(row == col, 1.0, 0.0)
    pairs = [(ci, h) for ci in range(n_chunks) for h in range(GDN_HEADS)]

    def rows(ci):
        return slice(ci * c, (ci + 1) * c)

    def cols(h):
        return slice(h * HEAD_DIM, (h + 1) * HEAD_DIM)

    q, k, kb, decay, eg, gc, rhs = {}, {}, {}, {}, {}, {}, {}
    for ci, h in pairs:
        g = (ci, h)
        q[g] = q_ref[0, rows(ci), cols(h)]
        k[g] = k_ref[0, rows(ci), cols(h)]
        beta = jnp.broadcast_to(val_ref[0, rows(ci), S_GB + h:S_GB + h + 1], (c, HEAD_DIM))
        gc[g] = jnp.broadcast_to(cs_ref[0, rows(ci), S_GA + h:S_GA + h + 1], (c, HEAD_DIM))
        gr = cst_ref[0, S_GA + h:S_GA + h + 1, rows(ci)]
        decay[g] = jnp.where(incl, jnp.exp(jnp.where(incl, gc[g][:, :c] - gr, 0.0)), 0.0)
        eg[g] = jnp.exp(gc[g])
        kb[g] = k[g] * beta
        rhs[g] = jnp.concatenate([v_ref[0, rows(ci), cols(h)] * beta, kb[g] * eg[g]], axis=1)
    a = {g: jnp.where(strict, _dot_nt(kb[g], k[g]) * decay[g], 0.0) for g in pairs}
    qk = {g: jnp.where(incl, _dot_nt(q[g], k[g]) * decay[g], 0.0) for g in pairs}
    off = (row // 2 == col // 2) & (row != col)
    tinv = {g: eye - jnp.where(off, a[g], 0.0) for g in pairs}
    sz = 2
    while sz < c:
        off = (row // (2 * sz) == col // (2 * sz)) & (row // sz != col // sz)
        left = {g: _dot(tinv[g], jnp.where(off, a[g], 0.0)) for g in pairs}
        tinv = {g: tinv[g] - _dot(left[g], tinv[g]) for g in pairs}
        sz *= 2
    sol = {g: _dot(tinv[g], rhs[g]) for g in pairs}
    state = [state_ref[h] for h in range(GDN_HEADS)]
    heads = range(GDN_HEADS)
    for ci in range(n_chunks):
        ws = [_dot(sol[ci, h][:, HEAD_DIM:], state[h]) for h in heads]
        qs = [_dot(q[ci, h] * eg[ci, h], state[h]) for h in heads]
        v_new = [sol[ci, h][:, :HEAD_DIM] - ws[h] for h in heads]
        intra = [_dot(qk[ci, h], v_new[h]) for h in heads]
        for h in heads:
            g_last = gc[ci, h][c - 1:c, :]
            k_dec = k[ci, h] * jnp.exp(g_last - gc[ci, h])
            state[h] = state[h] * jnp.exp(g_last) + _dot_tn(k_dec, v_new[h])
        for h in heads:
            o = _rms(qs[h] + intra[h], gn_ref[...]) * _silu(z_ref[0, rows(ci), cols(h)])
            o_ref[0, rows(ci), cols(h)] = o.astype(o_ref.dtype)
    for h in heads:
        state_ref[h] = state[h]


def _gdn(cv, p3, val, cs, cst, gain):
    b, s, _ = cv.shape
    t = _tile(s, GDN_ROWS)
    wide = lambda blk: pl.BlockSpec((1, t, GDN_W), lambda bi, i: (bi, i, blk))
    small = pl.BlockSpec((1, t, LANES), lambda bi, i: (bi, i, 0))
    return pl.pallas_call(
        _gdn_kernel,
        out_shape=jax.ShapeDtypeStruct((b, s, GDN_W), BF16),
        grid=(b, s // t),
        in_specs=[wide(C_GQ // GDN_W), wide(C_GK // GDN_W), wide(C_GV // GDN_W),
                  wide(P_GZ // GDN_W), small, small,
                  pl.BlockSpec((1, LANES, t), lambda bi, i: (bi, 0, i)),
                  pl.BlockSpec((1, HEAD_DIM), lambda bi, i: (0, 0))],
        out_specs=pl.BlockSpec((1, t, GDN_W), lambda bi, i: (bi, i, 0)),
        scratch_shapes=[pltpu.VMEM((GDN_HEADS, HEAD_DIM, HEAD_DIM), F32)],
        compiler_params=_params(("parallel", "arbitrary")),
        name="gated_deltanet",
    )(cv, cv, cv, p3, val, cs, cst, gain)


SSM_PAIRS = SSM_HEADS // 2
PAIRS_PER_GROUP = SSM_PAIRS // SSM_GROUPS


def _ssd_kernel(x_ref, b_ref, c_ref, z_ref, val_ref, cs_ref, cst_ref, par_ref, o_ref, st_ref):
    @pl.when(pl.program_id(1) == 0)
    def _():
        st_ref[...] = jnp.zeros_like(st_ref)

    n = SSM_CHUNK
    row = lax.broadcasted_iota(I32, (n, n), 0)
    col = lax.broadcasted_iota(I32, (n, n), 1)
    incl = row >= col
    lo = lax.broadcasted_iota(I32, (n, LANES), 1) < SSM_HEAD_DIM
    groups = range(SSM_GROUPS)
    pairs = range(SSM_PAIRS)
    lanes_of = lambda p: slice(p * LANES, (p + 1) * LANES)
    grp = lambda p: p // PAIRS_PER_GROUP
    st = [st_ref[p] for p in pairs]

    for ci in range(x_ref.shape[1] // n):
        rs = slice(ci * n, (ci + 1) * n)

        def decay_of(h):
            ac = jnp.broadcast_to(cs_ref[0, rs, S_DT + h:S_DT + h + 1], (n, LANES))
            ar = cst_ref[0, S_DT + h:S_DT + h + 1, rs]
            return jnp.where(incl, jnp.exp(jnp.where(incl, ac - ar, 0.0)), 0.0), ac

        bg = [b_ref[0, rs, g * SSM_STATE:(g + 1) * SSM_STATE] for g in groups]
        cg = [c_ref[0, rs, g * SSM_STATE:(g + 1) * SSM_STATE] for g in groups]
        cb = [_dot_nt(cg[g], bg[g]) for g in groups]
        bg_t = [bg[g].T for g in groups]
        xp, xdt, dec, e_ac, tail, st_dec = [], [], [], [], [], []
        for p in pairs:
            h0, h1 = 2 * p, 2 * p + 1
            xp.append(x_ref[0, rs, lanes_of(p)])
            dt = jnp.where(lo, val_ref[0, rs, S_DT + h0:S_DT + h0 + 1],
                           val_ref[0, rs, S_DT + h1:S_DT + h1 + 1])
            xdt.append(xp[p] * dt)
            d0, ac0 = decay_of(h0)
            d1, ac1 = decay_of(h1)
            dec.append((d0, d1))
            ac = jnp.where(lo, ac0, ac1)
            ac_last = ac[n - 1:n, :]
            e_ac.append(jnp.exp(ac))
            tail.append(jnp.exp(ac_last - ac) * xdt[p])
            st_dec.append(jnp.exp(ac_last))
        diag0 = [_dot(cb[grp(p)] * dec[p][0], xdt[p]) for p in pairs]
        diag1 = [_dot(cb[grp(p)] * dec[p][1], xdt[p]) for p in pairs]
        y_off = [_dot(cg[grp(p)], st[p]) for p in pairs]
        st_add = [_dot(bg_t[grp(p)], tail[p]) for p in pairs]
        ys = []
        for p in pairs:
            st[p] = st[p] * st_dec[p] + st_add[p]
            y_diag = jnp.where(lo, diag0[p], diag1[p])
            ys.append((y_diag + y_off[p] * e_ac[p] + par_ref[0:1, lanes_of(p)] * xp[p])
                      * _silu(z_ref[0, rs, lanes_of(p)]))
        for g in groups:
            mine = [p for p in pairs if grp(p) == g]
            ssq = sum(jnp.sum(ys[p] * ys[p], axis=-1, keepdims=True) for p in mine)
            inv = lax.rsqrt(ssq / (PAIRS_PER_GROUP * LANES) + NORM_EPS)
            for p in mine:
                o_ref[0, rs, lanes_of(p)] = (ys[p] * inv * par_ref[1:2, lanes_of(p)]).astype(o_ref.dtype)
    for p in pairs:
        st_ref[p] = st[p]


def _ssd(cv, p3, val, cs, cst, par):
    b, s, _ = cv.shape
    t = _tile(s, SSD_ROWS)
    small = pl.BlockSpec((1, t, LANES), lambda bi, i: (bi, i, 0))
    return pl.pallas_call(
        _ssd_kernel,
        out_shape=jax.ShapeDtypeStruct((b, s, SSM_W), BF16),
        grid=(b, s // t),
        in_specs=[pl.BlockSpec((1, t, SSM_W), lambda bi, i: (bi, i, C_SX // SSM_W)),
                  pl.BlockSpec((1, t, BC_W), lambda bi, i: (bi, i, C_SB // BC_W)),
                  pl.BlockSpec((1, t, BC_W), lambda bi, i: (bi, i, C_SC // BC_W)),
                  pl.BlockSpec((1, t, SSM_W), lambda bi, i: (bi, i, P_SZ // SSM_W)),
                  small, small,
                  pl.BlockSpec((1, LANES, t), lambda bi, i: (bi, 0, i)),
                  pl.BlockSpec((8, SSM_W), lambda bi, i: (0, 0))],
        out_specs=pl.BlockSpec((1, t, SSM_W), lambda bi, i: (bi, i, 0)),
        scratch_shapes=[pltpu.VMEM((SSM_PAIRS, SSM_STATE, LANES), F32)],
        compiler_params=_params(("parallel", "arbitrary")),
        name="ssd_scan",
    )(cv, cv, cv, p3, val, cs, cst, par)


def _outproj_kernel(a1_ref, a2_ref, a3_ref, w_ref, x_ref, o_ref):
    r1 = a1_ref.shape[1]
    r2 = r1 + a2_ref.shape[1]
    o_ref[...] = (x_ref[...] + _dot(a1_ref[...], w_ref[0, :r1, :]) + _dot(a2_ref[...], w_ref[0, r1:r2, :])
                  + _dot(a3_ref[...], w_ref[0, r2:, :]))


def _outproj(a1, a2, a3, w, layer, x):
    n, d = x.shape
    tm = _tile(n, 1024)
    tn = _tile(d, 1024)
    act = lambda a: pl.BlockSpec((tm, a.shape[1]), lambda i, j: (i, 0))
    return pl.pallas_call(
        _outproj_kernel,
        out_shape=jax.ShapeDtypeStruct((n, d), F32),
        grid=(n // tm, d // tn),
        in_specs=[act(a1), act(a2), act(a3),
                  pl.BlockSpec((1, w.shape[1], tn), lambda i, j: (layer, 0, j)),
                  pl.BlockSpec((tm, tn), lambda i, j: (i, j))],
        out_specs=pl.BlockSpec((tm, tn), lambda i, j: (i, j)),
        compiler_params=_params(("parallel", "parallel")),
        name="out_proj",
    )(a1, a2, a3, w, x)


def _ffn_kernel(x_ref, g_ref, wg_ref, wu_ref, wd_ref, o_ref, h_ref):
    @pl.when(pl.program_id(1) == 0)
    def _():
        x = x_ref[...]
        h_ref[...] = _rms(x, g_ref[...]).astype(BF16)
        o_ref[...] = x

    h = h_ref[...]
    act = _silu(_dot(h, wg_ref[0])) * _dot(h, wu_ref[0])
    o_ref[...] += _dot(act.astype(BF16), wd_ref[0])


def _ffn(x, gain, wg, wu, wd, layer):
    n, d = x.shape
    f = wg.shape[2]
    tm = _tile(n, 1024)
    tf = _tile(f, 512)
    return pl.pallas_call(
        _ffn_kernel,
        out_shape=jax.ShapeDtypeStruct((n, d), F32),
        grid=(n // tm, f // tf),
        in_specs=[pl.BlockSpec((tm, d), lambda i, j: (i, 0)),
                  pl.BlockSpec((1, d), lambda i, j: (0, 0)),
                  pl.BlockSpec((1, d, tf), lambda i, j: (layer, 0, j)),
                  pl.BlockSpec((1, d, tf), lambda i, j: (layer, 0, j)),
                  pl.BlockSpec((1, tf, d), lambda i, j: (layer, j, 0))],
        out_specs=pl.BlockSpec((tm, d), lambda i, j: (i, 0)),
        scratch_shapes=[pltpu.VMEM((tm, d), BF16)],
        compiler_params=_params(("parallel", "arbitrary")),
        name="dense_ffn",
    )(x, gain, wg, wu, wd)


def _router_kernel(x_ref, g_ref, r_ref, idx_ref, w_ref):
    h = _rms(x_ref[...], g_ref[...])
    r = r_ref[...]
    h_hi = h.astype(BF16)
    r_hi = r.astype(BF16)
    h_lo = (h - h_hi.astype(F32)).astype(BF16)
    r_lo = (r - r_hi.astype(F32)).astype(BF16)
    logits = _dot(h_hi, r_hi) + (_dot(h_hi, r_lo) + _dot(h_lo, r_hi))
    lane = lax.broadcasted_iota(I32, logits.shape, 1)
    l1 = jnp.where(lane < N_EXPERTS, logits, NEG_BIG)
    m1 = jnp.max(l1, axis=-1, keepdims=True)
    i1 = jnp.min(jnp.where(l1 == m1, lane, LANES), axis=-1, keepdims=True)
    l2 = jnp.where(lane == i1, NEG_BIG, l1)
    m2 = jnp.max(l2, axis=-1, keepdims=True)
    i2 = jnp.min(jnp.where(l2 == m2, lane, LANES), axis=-1, keepdims=True)
    e = jnp.exp(m2 - m1)
    idx_ref[...] = jnp.where(lane == 0, i1, jnp.where(lane == 1, i2, 0))
    w_ref[...] = jnp.where(lane == 0, 1.0 / (1.0 + e), jnp.where(lane == 1, e / (1.0 + e), 0.0))


def _router(x, gain, router_pad):
    n, d = x.shape
    tm = _tile(n, 512)
    return pl.pallas_call(
        _router_kernel,
        out_shape=(jax.ShapeDtypeStruct((n, LANES), I32), jax.ShapeDtypeStruct((n, LANES), F32)),
        grid=(n // tm,),
        in_specs=[pl.BlockSpec((tm, d), lambda i: (i, 0)),
                  pl.BlockSpec((1, d), lambda i: (0, 0)),
                  pl.BlockSpec((d, LANES), lambda i: (0, 0))],
        out_specs=(pl.BlockSpec((tm, LANES), lambda i: (i, 0)),
                   pl.BlockSpec((tm, LANES), lambda i: (i, 0))),
        compiler_params=_params(("parallel",)),
        name="moe_router",
    )(x, gain, router_pad)


def _row_copy(src_hbm, src_row, dst, dst_row, sem):
    return pltpu.make_async_copy(src_hbm.at[pl.ds(src_row, 1)], dst.at[pl.ds(dst_row, 1)], sem)


DMA_UNROLL = 8


def _for_rows(n, fn):
    assert n % DMA_UNROLL == 0

    def group(g, carry):
        base = pl.multiple_of(g * DMA_UNROLL, DMA_UNROLL)
        for u in range(DMA_UNROLL):
            fn(base + u)
        return carry

    lax.fori_loop(0, n // DMA_UNROLL, group, 0)


def _dispatch_kernel(d0_ref, d1_ref, pad_ref, x_ref, g_ref, xs_hbm, hbuf, sems):
    i = pl.program_id(0)
    last = pl.num_programs(0) - 1
    rows = x_ref.shape[0]
    n_fill = pad_ref.shape[2]
    slot = lax.rem(i, 2)

    def wait_block(s):
        def body(r):
            _row_copy(hbuf.at[s], r, xs_hbm, 0, sems.at[s]).wait()
            _row_copy(hbuf.at[s], r, xs_hbm, 0, sems.at[s]).wait()

        def fill_body(r):
            _row_copy(hbuf.at[s], 0, xs_hbm, 0, sems.at[s]).wait()
        _for_rows(rows, body)
        _for_rows(n_fill, fill_body)

    @pl.when(i >= 2)
    def _():
        wait_block(slot)

    hbuf[slot] = _rms(x_ref[...], g_ref[...])

    def start(r):
        _row_copy(hbuf.at[slot], r, xs_hbm, d0_ref[0, 0, r], sems.at[slot]).start()
        _row_copy(hbuf.at[slot], r, xs_hbm, d1_ref[0, 0, r], sems.at[slot]).start()

    def start_fill(r):
        _row_copy(hbuf.at[slot], 0, xs_hbm, pad_ref[0, 0, r], sems.at[slot]).start()

    _for_rows(rows, start)
    _for_rows(n_fill, start_fill)

    @pl.when(i == last)
    def _():
        wait_block(slot)

    @pl.when((i == last) & (i >= 1))
    def _():
        wait_block(1 - slot)


def _dispatch(x, gain, dest0, dest1, pad_slots, n_slots):
    n, d = x.shape
    tb = _tile(n, 256)
    nb = n // tb
    n_fill = pad_slots.shape[0] // nb
    assert n_fill * nb == pad_slots.shape[0]
    dspec = pl.BlockSpec((1, 1, tb), lambda i: (i, 0, 0), memory_space=pltpu.SMEM)
    return pl.pallas_call(
        _dispatch_kernel,
        out_shape=jax.ShapeDtypeStruct((n_slots, d), F32),
        grid=(nb,),
        in_specs=[dspec, dspec,
                  pl.BlockSpec((1, 1, n_fill), lambda i: (i, 0, 0), memory_space=pltpu.SMEM),
                  pl.BlockSpec((tb, d), lambda i: (i, 0)),
                  pl.BlockSpec((1, d), lambda i: (0, 0))],
        out_specs=pl.BlockSpec(memory_space=pl.ANY),
        scratch_shapes=[pltpu.VMEM((2, tb, d), F32), pltpu.SemaphoreType.DMA((2,))],
        compiler_params=_params(("arbitrary",)),
        name="moe_dispatch",
    )(dest0.reshape(nb, 1, tb), dest1.reshape(nb, 1, tb), pad_slots.reshape(nb, 1, n_fill), x, gain)


def _expert_ffn_kernel(e_ref, na_ref, x_ref, wg_ref, wu_ref, wd_ref, o_ref, xb_ref):
    s = pl.program_id(0)
    f = pl.program_id(1)

    @pl.when(f == 0)
    def _():
        o_ref[...] = jnp.zeros_like(o_ref)
        xb_ref[...] = x_ref[...].astype(BF16)

    @pl.when(s < na_ref[0])
    def _():
        x = xb_ref[...]
        gate = _dot(x, wg_ref[0, 0].astype(BF16))
        up = _dot(x, wu_ref[0, 0].astype(BF16))
        o_ref[...] += _dot((_silu(gate) * up).astype(BF16), wd_ref[0, 0].astype(BF16))


def _expert_ffn(xs, blk_expert, n_active, wg, wu, wd, layer):
    n_slots, d = xs.shape
    f = wg.shape[3]
    nblk = n_slots // MOE_ROWS
    tf = _tile(f, 256)
    nf = f // tf

    def fidx(s, j, na_ref):
        return jnp.where(s < na_ref[0], j, nf - 1)

    grid_spec = pltpu.PrefetchScalarGridSpec(
        num_scalar_prefetch=2,
        grid=(nblk, nf),
        in_specs=[pl.BlockSpec((MOE_ROWS, d), lambda s, j, e, na: (s, 0)),
                  pl.BlockSpec((1, 1, d, tf), lambda s, j, e, na: (layer, e[s], 0, fidx(s, j, na))),
                  pl.BlockSpec((1, 1, d, tf), lambda s, j, e, na: (layer, e[s], 0, fidx(s, j, na))),
                  pl.BlockSpec((1, 1, tf, d), lambda s, j, e, na: (layer, e[s], fidx(s, j, na), 0))],
        out_specs=pl.BlockSpec((MOE_ROWS, d), lambda s, j, e, na: (s, 0)),
        scratch_shapes=[pltpu.VMEM((MOE_ROWS, d), BF16)],
    )
    return pl.pallas_call(
        _expert_ffn_kernel,
        out_shape=jax.ShapeDtypeStruct((n_slots, d), F32),
        grid_spec=grid_spec,
        compiler_params=_params(("arbitrary", "arbitrary")),
        name="moe_expert_ffn",
    )(blk_expert, n_active, xs, wg, wu, wd)


def _combine_kernel(d0_ref, d1_ref, d0_next_ref, d1_next_ref, x_ref, w_ref, g_ref, y_hbm, o_ref,
                    buf, sems, *, final_norm):
    i = pl.program_id(0)
    rows = o_ref.shape[0]
    slot = lax.rem(i, 2)

    def start_block(a_ref, b_ref, s):
        def body(r):
            _row_copy(y_hbm, a_ref[0, 0, r], buf.at[s, 0], r, sems.at[s]).start()
            _row_copy(y_hbm, b_ref[0, 0, r], buf.at[s, 1], r, sems.at[s]).start()
        _for_rows(rows, body)

    @pl.when(i == 0)
    def _():
        start_block(d0_ref, d1_ref, 0)

    @pl.when(i + 1 < pl.num_programs(0))
    def _():
        start_block(d0_next_ref, d1_next_ref, 1 - slot)

    def wait(r):
        _row_copy(y_hbm, 0, buf.at[slot, 0], r, sems.at[slot]).wait()
        _row_copy(y_hbm, 0, buf.at[slot, 1], r, sems.at[slot]).wait()

    _for_rows(rows, wait)
    out = x_ref[...] + w_ref[:, 0:1] * buf[slot, 0] + w_ref[:, 1:2] * buf[slot, 1]
    o_ref[...] = _rms(out, g_ref[...]) if final_norm else out


def _combine(x, top_w, dest0, dest1, ys, final_gain=None):
    n, d = x.shape
    tb = _tile(n, 256)
    nb = n // tb
    dspec = pl.BlockSpec((1, 1, tb), lambda i: (i, 0, 0), memory_space=pltpu.SMEM)
    dnext = pl.BlockSpec((1, 1, tb), lambda i: (jnp.minimum(i + 1, nb - 1), 0, 0),
                         memory_space=pltpu.SMEM)
    d0 = dest0.reshape(nb, 1, tb)
    d1 = dest1.reshape(nb, 1, tb)
    gain = jnp.ones((1, d), F32) if final_gain is None else final_gain
    return pl.pallas_call(
        functools.partial(_combine_kernel, final_norm=final_gain is not None),
        out_shape=jax.ShapeDtypeStruct((n, d), F32),
        grid=(nb,),
        in_specs=[dspec, dspec, dnext, dnext,
                  pl.BlockSpec((tb, d), lambda i: (i, 0)),
                  pl.BlockSpec((tb, LANES), lambda i: (i, 0)),
                  pl.BlockSpec((1, d), lambda i: (0, 0)),
                  pl.BlockSpec(memory_space=pl.ANY)],
        out_specs=pl.BlockSpec((tb, d), lambda i: (i, 0)),
        scratch_shapes=[pltpu.VMEM((2, 2, tb, d), F32), pltpu.SemaphoreType.DMA((2,))],
        compiler_params=_params(("arbitrary",)),
        name="moe_combine",
    )(d0, d1, d0, d1, x, top_w, gain, ys)


def _moe(x, gain, router, wg, wu, wd, layer, final_gain=None):
    n, d = x.shape
    router_pad = jnp.pad(router, ((0, 0), (0, LANES - N_EXPERTS)))
    top_idx, top_w = _router(x, gain, router_pad)
    flat_e = top_idx[:, :TOP_K].reshape(-1)
    onehot = (flat_e[:, None] == jnp.arange(N_EXPERTS, dtype=I32)[None, :]).astype(I32)
    rank = jnp.take_along_axis(jnp.cumsum(onehot, axis=0) - onehot, flat_e[:, None], axis=1)[:, 0]
    counts = jnp.sum(onehot, axis=0)
    padded = (counts + MOE_ROWS - 1) // MOE_ROWS * MOE_ROWS
    pad_end = jnp.cumsum(padded)
    dest = (pad_end - padded)[flat_e] + rank
    nblk = (n * TOP_K) // MOE_ROWS + N_EXPERTS
    n_slots = nblk * MOE_ROWS
    n_active = (pad_end[-1] // MOE_ROWS).astype(I32)
    blk_start = jnp.arange(nblk, dtype=I32) * MOE_ROWS
    count_le = lambda edges, v: jnp.sum((edges[None, :] <= v[:, None]).astype(I32), axis=1)
    blk_expert = jnp.minimum(count_le(pad_end, blk_start), N_EXPERTS - 1)
    last_e = blk_expert[jnp.maximum(n_active - 1, 0)]
    blk_expert = jnp.where(jnp.arange(nblk) < n_active, blk_expert, last_e).astype(I32)

    pad_sizes = jnp.concatenate([padded - counts, (n_slots - pad_end[-1])[None]])
    pad_first = jnp.concatenate([pad_end - padded + counts, pad_end[-1:]])
    pad_cum = jnp.cumsum(pad_sizes)
    k = jnp.arange(n_slots - n * TOP_K, dtype=I32)
    seg = count_le(pad_cum, k)
    pad_slots = (pad_first[seg] + k - (pad_cum - pad_sizes)[seg]).astype(I32)

    dest2 = dest.reshape(n, TOP_K)
    xs = _dispatch(x, gain, dest2[:, 0], dest2[:, 1], pad_slots, n_slots)
    ys = _expert_ffn(xs, blk_expert, n_active.reshape(1), wg, wu, wd, layer)
    return _combine(x, top_w, dest2[:, 0], dest2[:, 1], ys, final_gain)


def _final_norm_kernel(x_ref, g_ref, o_ref):
    o_ref[...] = _rms(x_ref[...], g_ref[...])


def _final_norm(x, gain):
    n, d = x.shape
    tm = _tile(n, 512)
    return pl.pallas_call(
        _final_norm_kernel,
        out_shape=jax.ShapeDtypeStruct((n, d), F32),
        grid=(n // tm,),
        in_specs=[pl.BlockSpec((tm, d), lambda i: (i, 0)), pl.BlockSpec((1, d), lambda i: (0, 0))],
        out_specs=pl.BlockSpec((tm, d), lambda i: (i, 0)),
        compiler_params=_params(("parallel",)),
        name="final_norm",
    )(x, gain)


def _layout_w_in(w_in):
    offs = [0]
    for sz in PROJ_SIZES:
        offs.append(offs[-1] + sz)
    seg = [w_in[..., offs[k]:offs[k + 1]] for k in range(len(PROJ_SIZES))]
    fq, fk, fv, ff, gqkv, gz, gb, ga, sz_, sxbc, sdt = seg
    lead = w_in.shape[:-1]
    small = jnp.concatenate([ff, gb, ga, sdt, jnp.zeros(lead + (LANES - S_END,), w_in.dtype)], -1)
    tail = jnp.zeros(lead + (P_WIDTH - P_SMALL - LANES,), w_in.dtype)
    if w_in.ndim == 3:
        n_layers, rows, width = w_in.shape
        tr = _tile(rows, 256)

        def body(w_ref, o_ref):
            o_ref[0] = _layout_w_in(w_ref[0])

        return pl.pallas_call(
            body, out_shape=jax.ShapeDtypeStruct((n_layers, rows, P_WIDTH), BF16),
            grid=(n_layers, rows // tr),
            in_specs=[pl.BlockSpec((1, tr, width), lambda l, i: (l, i, 0))],
            out_specs=pl.BlockSpec((1, tr, P_WIDTH), lambda l, i: (l, i, 0)),
            compiler_params=_params(("parallel", "parallel")), name="w_in_layout")(w_in)
    return jnp.concatenate([fq, fk, fv, gqkv, gz, sz_, sxbc, small, tail], -1).astype(BF16)


def _row_tile(rows, width):
    out = jnp.zeros((8, width), F32)
    for r, v in enumerate(rows):
        out = out.at[r, :v.shape[0]].set(v.astype(F32))
    return out


def kernel(x, norm_mix, w_in, fox_f_bias, fox_out_norm, gdn_conv_w, gdn_A_log, gdn_dt_bias,
           gdn_out_norm, ssm_conv_w, ssm_conv_b, ssm_A_log, ssm_dt_bias, ssm_D, ssm_out_norm,
           w_out, norm_ffn, ffn_w_gate, ffn_w_up, ffn_w_down, moe_router, moe_w_gate,
           moe_w_up, moe_w_down, norm_final):
    b, s, d = x.shape
    depth = w_in.shape[0]
    n = b * s
    w_in_l = _layout_w_in(w_in)
    w_out_b = w_out.astype(BF16)
    ffn_g, ffn_u, ffn_d = (w.astype(BF16) for w in (ffn_w_gate, ffn_w_up, ffn_w_down))
    conv_w = jnp.concatenate([gdn_conv_w, ssm_conv_w], axis=-1)
    conv_b = jnp.concatenate([jnp.zeros((depth, 3 * GDN_W), F32), ssm_conv_b], axis=-1)

    xf = x.reshape(n, d)
    for layer in range(depth):
        p = _norm_matmul(xf, norm_mix[layer][None, :], w_in_l, layer)
        p3 = p.reshape(b, s, P_WIDTH)
        zeros = lambda k: jnp.zeros((k,), F32)
        bias_row = jnp.concatenate([fox_f_bias[layer], zeros(GDN_HEADS), gdn_dt_bias[layer],
                                    ssm_dt_bias[layer]])
        alog_row = jnp.concatenate([zeros(S_GA), gdn_A_log[layer], ssm_A_log[layer]])
        val, cs, cst = _prep(p3, _row_tile([bias_row, alog_row], LANES))
        cv = _conv(p3, conv_w[layer], conv_b[layer][None, :])
        o_fox = _fox(p3, cst, fox_out_norm[layer].reshape(1, FOX_W))
        o_gdn = _gdn(cv, p3, val, cs, cst, gdn_out_norm[layer][None, :])
        ssm_par = _row_tile([jnp.repeat(ssm_D[layer], SSM_HEAD_DIM), ssm_out_norm[layer]], SSM_W)
        o_ssm = _ssd(cv, p3, val, cs, cst, ssm_par)
        xf = _outproj(o_fox.reshape(n, FOX_W), o_gdn.reshape(n, GDN_W), o_ssm.reshape(n, SSM_W),
                      w_out_b, layer, xf)
        j = layer // 2
        gain = norm_ffn[layer][None, :]
        if layer % 2 == 0:
            xf = _ffn(xf, gain, ffn_g, ffn_u, ffn_d, j)
        else:
            fused_final = norm_final[None, :] if layer == depth - 1 else None
            xf = _moe(xf, gain, moe_router[j], moe_w_gate, moe_w_up, moe_w_down, j, fused_final)
    if depth % 2:
        xf = _final_norm(xf, norm_final[None, :])
    return xf.reshape(b, s, d)
```
